```python
import jax, jax.numpy as jnp
from jax import lax
import numpy as np


D_MODEL = 1024
BATCH = 8
SEQ = 2048
DEPTH = 1
DEC_BATCH = 32
DEC_SEQ = 1
PAST_LEN = 16384
PAGE_SIZE = 128

HEAD_DIM = 64
N_HEADS_MOBA = 8
N_HEADS_FOX = 8
W_MOBA = N_HEADS_MOBA * HEAD_DIM
W_FOX = N_HEADS_FOX * HEAD_DIM
MOBA_BLOCK = 256
MOBA_TOPK = 3
Q_BLOCK = 128
ROPE_THETA = 500000.0
ROPE_DIMS = HEAD_DIM // 4
D_FF = 3 * D_MODEL
CONV_W = 3
PLE_DIM = 256
EPS = 1e-6
D_IN = 3 * W_MOBA + 3 * W_FOX + N_HEADS_FOX + 2 * D_MODEL

kernel_name = "moba_fox_parallel_convffn_decoder_step"


def rmsnorm(x, g):
    x32 = x.astype(jnp.float32)
    y = x32 * lax.rsqrt(jnp.mean(x32 * x32, axis=-1, keepdims=True) + EPS)
    return (y * g.astype(jnp.float32)).astype(x.dtype)


def partial_rope(x, pos):
    half = ROPE_DIMS // 2
    inv = ROPE_THETA ** (-jnp.arange(0, ROPE_DIMS, 2, dtype=jnp.float32) / ROPE_DIMS)
    ang = pos.astype(jnp.float32)[:, None] * inv[None, :]
    cos = jnp.cos(ang)[None, :, None, :]
    sin = jnp.sin(ang)[None, :, None, :]
    x32 = x.astype(jnp.float32)
    x1 = x32[..., :half]
    x2 = x32[..., half:ROPE_DIMS]
    out = jnp.concatenate([x1 * cos - x2 * sin, x2 * cos + x1 * sin, x32[..., ROPE_DIMS:]], axis=-1)
    return out.astype(x.dtype)


def mixer_inputs(x, pos, norm_attn_g, w_in, b_forget, qn_a, kn_a, qn_b, kn_b):
    B, T, _ = x.shape
    h = rmsnorm(x, norm_attn_g)
    proj = h @ w_in
    sizes = [W_MOBA] * 3 + [W_FOX] * 3 + [N_HEADS_FOX, D_MODEL, D_MODEL]
    cuts = [int(c) for c in np.cumsum(sizes)[:-1]]
    qa, ka, va, qb, kb, vb, fl, ga, gb = jnp.split(proj, cuts, axis=-1)
    qa = partial_rope(rmsnorm(qa.reshape(B, T, N_HEADS_MOBA, HEAD_DIM), qn_a), pos)
    ka = partial_rope(rmsnorm(ka.reshape(B, T, N_HEADS_MOBA, HEAD_DIM), kn_a), pos)
    va = va.reshape(B, T, N_HEADS_MOBA, HEAD_DIM)
    qb = rmsnorm(qb.reshape(B, T, N_HEADS_FOX, HEAD_DIM), qn_b)
    kb = rmsnorm(kb.reshape(B, T, N_HEADS_FOX, HEAD_DIM), kn_b)
    vb = vb.reshape(B, T, N_HEADS_FOX, HEAD_DIM)
    lf = jax.nn.log_sigmoid((fl + b_forget).astype(jnp.float32))
    return qa, ka, va, qb, kb, vb, lf, ga, gb


def moba_select(q, qpos, kmean):
    nb = kmean.shape[2]
    k_sel = min(MOBA_TOPK, nb)
    s = jnp.einsum('bqhd,bhnd->bhqn', q.astype(jnp.float32), kmean.astype(jnp.float32))
    own = qpos // MOBA_BLOCK
    past = jnp.arange(nb, dtype=jnp.int32)[None, :] < own[:, None]
    s = jnp.where(past[None, None], s, -jnp.inf)
    topv, topi = lax.top_k(s, k_sel)
    own_b = jnp.broadcast_to(own[None, None, :, None], topi.shape[:-1] + (1,)).astype(topi.dtype)
    idx = jnp.concatenate([topi, own_b], axis=-1)
    valid = jnp.concatenate([jnp.isfinite(topv), jnp.ones(own_b.shape, dtype=bool)], axis=-1)
    return idx, valid


def moba_attend(q, qpos, kg, vg, kpos, valid):
    scale = HEAD_DIM ** -0.5
    logits = jnp.einsum('bqhd,bhqjkd->bhqjk', q, kg).astype(jnp.float32) * scale
    mask = valid[..., None] & (kpos <= qpos[None, None, :, None, None])
    logits = jnp.where(mask, logits, -jnp.inf)
    B, H, Tq, J, BLK = logits.shape
    p = jax.nn.softmax(logits.reshape(B, H, Tq, J * BLK), axis=-1).reshape(B, H, Tq, J, BLK)
    return jnp.einsum('bhqjk,bhqjkd->bqhd', p.astype(vg.dtype), vg)


def moba_prompt_seq(qkv):
    q, k, v = qkv
    T, H, dh = q.shape
    nb = -(-T // MOBA_BLOCK)
    lpad = nb * MOBA_BLOCK
    kp = jnp.pad(k, ((0, lpad - T), (0, 0), (0, 0)))
    vp = jnp.pad(v, ((0, lpad - T), (0, 0), (0, 0)))
    kmean = kp.reshape(nb, MOBA_BLOCK, H, dh).mean(axis=1, dtype=jnp.float32).transpose(1, 0, 2)[None]
    h_idx = jnp.arange(H)[:, None, None, None]

    def chunk(j):
        qc = lax.dynamic_slice_in_dim(q, j * Q_BLOCK, Q_BLOCK, axis=0)[None]
        qpos = j * Q_BLOCK + jnp.arange(Q_BLOCK, dtype=jnp.int32)
        idx, valid = moba_select(qc, qpos, kmean)
        kpos = idx[..., None] * MOBA_BLOCK + jnp.arange(MOBA_BLOCK, dtype=jnp.int32)
        kg = kp[kpos[0], h_idx][None]
        vg = vp[kpos[0], h_idx][None]
        return moba_attend(qc, qpos, kg, vg, kpos, valid)[0]

    out = lax.map(chunk, jnp.arange(T // Q_BLOCK, dtype=jnp.int32))
    return out.reshape(T, H, dh)


def fetch_rows(pool, new, page_table, kpos):
    DB, n_pages = page_table.shape
    past_len = n_pages * PAGE_SIZE
    ts = new.shape[1]
    H = new.shape[2]
    b_idx = jnp.arange(DB)[:, None, None, None, None]
    h_idx = jnp.arange(H)[None, :, None, None, None]
    phys = page_table[b_idx, jnp.clip(kpos // PAGE_SIZE, 0, n_pages - 1)]
    past_rows = pool[phys, kpos % PAGE_SIZE, h_idx]
    new_rows = new[b_idx, jnp.clip(kpos - past_len, 0, ts - 1), h_idx]
    return jnp.where((kpos < past_len)[..., None], past_rows, new_rows)


def moba_sample(q, k_new, v_new, pool_k, pool_v, page_table):
    DB, TS, H, dh = q.shape
    past_len = page_table.shape[1] * PAGE_SIZE
    L = past_len + TS
    nb = -(-L // MOBA_BLOCK)
    lpad = nb * MOBA_BLOCK
    k_past = pool_k[page_table].reshape(DB, past_len, H, dh)
    k_full = jnp.pad(jnp.concatenate([k_past, k_new], axis=1), ((0, 0), (0, lpad - L), (0, 0), (0, 0)))
    kmean = k_full.reshape(DB, nb, MOBA_BLOCK, H, dh).mean(axis=2, dtype=jnp.float32).transpose(0, 2, 1, 3)
    qpos = past_len + jnp.arange(TS, dtype=jnp.int32)
    idx, valid = moba_select(q, qpos, kmean)
    kpos = idx[..., None] * MOBA_BLOCK + jnp.arange(MOBA_BLOCK, dtype=jnp.int32)
    kg = fetch_rows(pool_k, k_new, page_table, kpos)
    vg = fetch_rows(pool_v, v_new, page_table, kpos)
    return moba_attend(q, qpos, kg, vg, kpos, valid)


def fox_prompt(q, k, v, lf):
    B, T, H, dh = q.shape
    scale = HEAD_DIM ** -0.5
    c = lax.cumsum(lf, axis=1).transpose(0, 2, 1)
    kpos = jnp.arange(T, dtype=jnp.int32)

    def blk(j):
        qb = lax.dynamic_slice_in_dim(q, j * Q_BLOCK, Q_BLOCK, axis=1)
        cq = lax.dynamic_slice_in_dim(c, j * Q_BLOCK, Q_BLOCK, axis=2)
        qpos = j * Q_BLOCK + jnp.arange(Q_BLOCK, dtype=jnp.int32)
        logits = (jnp.einsum('bqhd,bshd->bhqs', qb, k).astype(jnp.float32) * scale
                  + cq[..., None] - c[:, :, None, :])
        logits = jnp.where(kpos[None, :] <= qpos[:, None], logits, -jnp.inf)
        p = jax.nn.softmax(logits, axis=-1).astype(v.dtype)
        return jnp.einsum('bhqs,bshd->bqhd', p, v)

    out = lax.map(blk, jnp.arange(T // Q_BLOCK, dtype=jnp.int32))
    return out.transpose(1, 0, 2, 3, 4).reshape(B, T, H, dh)


def fox_sample(q, k_new, v_new, lf_new, pool_k, pool_v, pool_lf, page_table):
    DB, TS, H, dh = q.shape
    past_len = page_table.shape[1] * PAGE_SIZE
    scale = HEAD_DIM ** -0.5
    k_past = pool_k[page_table].reshape(DB, past_len, H, dh)
    v_past = pool_v[page_table].reshape(DB, past_len, H, dh)
    lf_past = pool_lf[page_table].reshape(DB, past_len, H).astype(jnp.float32)
    cn = lax.cumsum(lf_new.astype(jnp.float32), axis=1).transpose(0, 2, 1)
    r_past = (lax.cumsum(lf_past, axis=1, reverse=True) - lf_past).transpose(0, 2, 1)
    lp = (jnp.einsum('bqhd,bshd->bhqs', q, k_past).astype(jnp.float32) * scale
          + cn[..., None] + r_past[:, :, None, :])
    ln = (jnp.einsum('bqhd,bshd->bhqs', q, k_new).astype(jnp.float32) * scale
          + cn[..., None] - cn[:, :, None, :])
    tpos = jnp.arange(TS, dtype=jnp.int32)
    ln = jnp.where(tpos[None, :] <= tpos[:, None], ln, -jnp.inf)
    p = jax.nn.softmax(jnp.concatenate([lp, ln], axis=-1), axis=-1).astype(v_past.dtype)
    return (jnp.einsum('bhqs,bshd->bqhd', p[..., :past_len], v_past)
            + jnp.einsum('bhqs,bshd->bqhd', p[..., past_len:], v_new))


def merge_ffn_ple(x, ya, yb, ga, gb, conv_buf, p, w_branch_moba, w_branch_fox, w_out,
                  norm_ffn_g, w_up, w_gate, conv_w, conv_b, w_down, norm_ple_g, w_ple, w_ple_gate):
    B, T, _ = x.shape
    merged = (jax.nn.sigmoid(ga) * (ya.reshape(B, T, W_MOBA) @ w_branch_moba)
              + jax.nn.sigmoid(gb) * (yb.reshape(B, T, W_FOX) @ w_branch_fox))
    x = x + merged @ w_out
    h = rmsnorm(x, norm_ffn_g)
    u = h @ w_up
    a = h @ w_gate
    ext = jnp.concatenate([conv_buf.astype(a.dtype), a], axis=1)
    conv = conv_b
    for i in range(CONV_W):
        conv = conv + conv_w[i] * ext[:, i:i + T]
    x = x + (jax.nn.silu(conv) * u) @ w_down
    gate = jax.nn.sigmoid(rmsnorm(x, norm_ple_g) @ w_ple_gate)
    x = x + (p @ w_ple) * gate
    return x, ext[:, T:]


def setup_inputs(seed: int = 0) -> dict:
    key = jax.random.key(seed)
    ks = jax.random.split(key, 40)
    f32 = jnp.float32
    n_pages = PAST_LEN // PAGE_SIZE
    n_pool = (DEC_BATCH * n_pages * 5) // 4

    def nrm(k, shape, scale):
        return jax.random.normal(k, shape, f32) * scale

    def gain(k, shape):
        return 1.0 + 0.05 * jax.random.normal(k, shape, f32)

    page_table = jax.random.permutation(ks[0], n_pool)[:DEC_BATCH * n_pages].reshape(DEC_BATCH, n_pages).astype(jnp.int32)
    return {
        'x_prompt': nrm(ks[1], (BATCH, SEQ, D_MODEL), 1.0),
        'x_sample': nrm(ks[2], (DEC_BATCH, DEC_SEQ, D_MODEL), 1.0),
        'cache_moba_k': nrm(ks[3], (DEPTH, n_pool, PAGE_SIZE, N_HEADS_MOBA, HEAD_DIM), 1.0),
        'cache_moba_v': nrm(ks[4], (DEPTH, n_pool, PAGE_SIZE, N_HEADS_MOBA, HEAD_DIM), 1.0),
        'cache_fox_k': nrm(ks[5], (DEPTH, n_pool, PAGE_SIZE, N_HEADS_FOX, HEAD_DIM), 1.0),
        'cache_fox_v': nrm(ks[6], (DEPTH, n_pool, PAGE_SIZE, N_HEADS_FOX, HEAD_DIM), 1.0),
        'cache_fox_logf': jax.nn.log_sigmoid(2.0 + nrm(ks[7], (DEPTH, n_pool, PAGE_SIZE, N_HEADS_FOX), 0.5)),
        'state_ffn_conv': nrm(ks[8], (DEPTH, DEC_BATCH, CONV_W - 1, D_FF), 1.0),
        'page_table': page_table,
        'p_prompt': nrm(ks[9], (DEPTH, BATCH, SEQ, PLE_DIM), 1.0),
        'p_sample': nrm(ks[10], (DEPTH, DEC_BATCH, DEC_SEQ, PLE_DIM), 1.0),
        'norm_attn_g': gain(ks[11], (DEPTH, D_MODEL)),
        'w_in': nrm(ks[12], (DEPTH, D_MODEL, D_IN), D_MODEL ** -0.5),
        'b_forget': 2.0 + nrm(ks[13], (DEPTH, N_HEADS_FOX), 0.1),
        'qnorm_moba': gain(ks[14], (DEPTH, HEAD_DIM)),
        'knorm_moba': gain(ks[15], (DEPTH, HEAD_DIM)),
        'qnorm_fox': gain(ks[16], (DEPTH, HEAD_DIM)),
        'knorm_fox': gain(ks[17], (DEPTH, HEAD_DIM)),
        'w_branch_moba': nrm(ks[18], (DEPTH, W_MOBA, D_MODEL), W_MOBA ** -0.5),
        'w_branch_fox': nrm(ks[19], (DEPTH, W_FOX, D_MODEL), W_FOX ** -0.5),
        'w_out': nrm(ks[20], (DEPTH, D_MODEL, D_MODEL), D_MODEL ** -0.5),
        'norm_ffn_g': gain(ks[21], (DEPTH, D_MODEL)),
        'w_up': nrm(ks[22], (DEPTH, D_MODEL, D_FF), D_MODEL ** -0.5),
        'w_gate': nrm(ks[23], (DEPTH, D_MODEL, D_FF), D_MODEL ** -0.5),
        'conv_w': nrm(ks[24], (DEPTH, CONV_W, D_FF), CONV_W ** -0.5),
        'conv_b': nrm(ks[25], (DEPTH, D_FF), 0.02),
        'w_down': nrm(ks[26], (DEPTH, D_FF, D_MODEL), D_FF ** -0.5),
        'norm_ple_g': gain(ks[27], (DEPTH, D_MODEL)),
        'w_ple': nrm(ks[28], (DEPTH, PLE_DIM, D_MODEL), PLE_DIM ** -0.5),
        'w_ple_gate': nrm(ks[29], (DEPTH, D_MODEL, D_MODEL), D_MODEL ** -0.5),
    }


def reference(x_prompt, x_sample, cache_moba_k, cache_moba_v, cache_fox_k, cache_fox_v, cache_fox_logf,
              state_ffn_conv, page_table, p_prompt, p_sample, norm_attn_g, w_in, b_forget,
              qnorm_moba, knorm_moba, qnorm_fox, knorm_fox, w_branch_moba, w_branch_fox, w_out,
              norm_ffn_g, w_up, w_gate, conv_w, conv_b, w_down, norm_ple_g, w_ple, w_ple_gate):
    B, T, _ = x_prompt.shape
    DB, TS, _ = x_sample.shape
    past_len = page_table.shape[1] * PAGE_SIZE
    pos_p = jnp.arange(T, dtype=jnp.int32)
    pos_s = past_len + jnp.arange(TS, dtype=jnp.int32)
    zero_buf = jnp.zeros((B, CONV_W - 1, D_FF), x_prompt.dtype)
    xp = x_prompt
    xs = x_sample
    mk_p, mv_p, fk_p, fv_p, fl_p, cv_p = [], [], [], [], [], []
    mk_s, mv_s, fk_s, fv_s, fl_s, cv_s = [], [], [], [], [], []
    for l in range(DEPTH):
        ffn_args = (w_branch_moba[l], w_branch_fox[l], w_out[l], norm_ffn_g[l], w_up[l], w_gate[l],
                    conv_w[l], conv_b[l], w_down[l], norm_ple_g[l], w_ple[l], w_ple_gate[l])
        qa, ka, va, qb, kb, vb, lf, ga, gb = mixer_inputs(
            xp, pos_p, norm_attn_g[l], w_in[l], b_forget[l], qnorm_moba[l], knorm_moba[l], qnorm_fox[l], knorm_fox[l])
        ya = lax.map(moba_prompt_seq, (qa, ka, va))
        yb = fox_prompt(qb, kb, vb, lf)
        xp, buf_p = merge_ffn_ple(xp, ya, yb, ga, gb, zero_buf, p_prompt[l], *ffn_args)
        mk_p.append(ka); mv_p.append(va); fk_p.append(kb); fv_p.append(vb); fl_p.append(lf); cv_p.append(buf_p)
        qa, ka, va, qb, kb, vb, lf, ga, gb = mixer_inputs(
            xs, pos_s, norm_attn_g[l], w_in[l], b_forget[l], qnorm_moba[l], knorm_moba[l], qnorm_fox[l], knorm_fox[l])
        ya = moba_sample(qa, ka, va, cache_moba_k[l], cache_moba_v[l], page_table)
        yb = fox_sample(qb, kb, vb, lf, cache_fox_k[l], cache_fox_v[l], cache_fox_logf[l], page_table)
        xs, buf_s = merge_ffn_ple(xs, ya, yb, ga, gb, state_ffn_conv[l], p_sample[l], *ffn_args)
        mk_s.append(ka); mv_s.append(va); fk_s.append(kb); fv_s.append(vb); fl_s.append(lf); cv_s.append(buf_s)
    return (xp, xs,
            jnp.stack(mk_p), jnp.stack(mv_p), jnp.stack(fk_p), jnp.stack(fv_p), jnp.stack(fl_p), jnp.stack(cv_p),
            jnp.stack(mk_s), jnp.stack(mv_s), jnp.stack(fk_s), jnp.stack(fv_s), jnp.stack(fl_s), jnp.stack(cv_s))
```

```python
import functools

import jax
import jax.numpy as jnp
import numpy as np
from jax import lax
from jax.experimental import pallas as pl
from jax.experimental.pallas import tpu as pltpu

F32 = jnp.float32
BF16 = jnp.bfloat16

HEAD_DIM = 64
N_HEADS = 8
W_ATT = N_HEADS * HEAD_DIM
MOBA_BLOCK = 256
MOBA_TOPK = 3
PAGE_SIZE = 128
ROPE_THETA = 500000.0
ROPE_DIMS = HEAD_DIM // 4
CONV_W = 3
EPS = 1e-6
SCALE = HEAD_DIM ** -0.5
NEG = -1e30

LANES = 128
SUBLANES = 8
ATT_TILE = MOBA_BLOCK
VMEM_LIMIT = 56 * 1024 * 1024


def _bf(x):
    return x.astype(BF16)


def _dot(a, b):
    return jnp.dot(a, b, preferred_element_type=F32)


def _dot_nt(a, b):
    return lax.dot_general(a, b, (((1,), (1,)), ((), ())), preferred_element_type=F32)


def _split2(x):
    hi = _bf(x)
    lo = _bf(x - hi.astype(F32))
    return hi, lo


def _split3(x):
    hi = _bf(x)
    r = x - hi.astype(F32)
    mid = _bf(r)
    lo = _bf(r - mid.astype(F32))
    return hi, mid, lo


def _dot_left_exact(a_bf, x):
    hi, mid, lo = _split3(x)
    return _dot(a_bf, hi) + _dot(a_bf, mid) + _dot(a_bf, lo)


def _dot_right_exact(x, b_bf):
    hi, mid, lo = _split3(x)
    return _dot(hi, b_bf) + _dot(mid, b_bf) + _dot(lo, b_bf)


def _sigmoid(x):
    return 1.0 / (1.0 + jnp.exp(-x))


def _rms(x):
    return x * lax.rsqrt(jnp.mean(x * x, axis=-1, keepdims=True) + EPS)


def _const_spec(shape):
    nd = len(shape)
    return pl.BlockSpec(shape, lambda *_: (0,) * nd, pipeline_mode=pl.Buffered(1))


def _params(sem):
    return pltpu.CompilerParams(dimension_semantics=sem, vmem_limit_bytes=VMEM_LIMIT)


def _inproj_body(x_ref, g_ref, w_ref, bfl_ref, gains_ref, rope_ref, seg_ref, segt_ref, tri_ref,
                 qa_ref, ka_ref, va_ref, qb_ref, kb_ref, vb_ref, lf_ref, c_ref, sga_ref, sgb_ref,
                 carry_ref, *, tiles_per_seq):
    t = pl.program_id(0)
    tm = x_ref.shape[0]
    d_model = x_ref.shape[1]
    h = _bf(_rms(x_ref[...]) * g_ref[...])
    seg = seg_ref[...]
    segt = segt_ref[...]
    rope = rope_ref[...]

    def widen(tbl):
        return jnp.concatenate([tbl] * (W_ATT // LANES), axis=1)

    cos_t = widen(rope[:, 0:LANES])
    sin_up = widen(rope[:, LANES:2 * LANES])
    sin_dn = widen(rope[:, 2 * LANES:3 * LANES])

    def proj(c0, n):
        return _dot(h, w_ref[:, c0:c0 + n])

    def headnorm(y, gain):
        ss = _dot(_bf(y * y), seg)
        r = lax.rsqrt(ss * (1.0 / HEAD_DIM) + EPS)
        r_hi, r_lo = _split2(r)
        return y * (_dot(r_hi, segt) + _dot(r_lo, segt)) * gain

    def rope_fn(y):
        half = ROPE_DIMS // 2
        return (y * cos_t + pltpu.roll(y, W_ATT - half, 1) * sin_up
                + pltpu.roll(y, half, 1) * sin_dn)

    qa_ref[...] = rope_fn(headnorm(proj(0, W_ATT), gains_ref[0:1, :]))
    ka_ref[...] = rope_fn(headnorm(proj(W_ATT, W_ATT), gains_ref[1:2, :]))
    va_ref[...] = proj(2 * W_ATT, W_ATT)
    qb_ref[...] = headnorm(proj(3 * W_ATT, W_ATT), gains_ref[2:3, :])
    kb_ref[...] = headnorm(proj(4 * W_ATT, W_ATT), gains_ref[3:4, :])
    vb_ref[...] = proj(5 * W_ATT, W_ATT)
    sga_ref[...] = _sigmoid(proj(6 * W_ATT, d_model))
    sgb_ref[...] = _sigmoid(proj(6 * W_ATT + d_model, d_model))

    fl = proj(6 * W_ATT + 2 * d_model, LANES) + bfl_ref[...]
    lf = jnp.minimum(fl, 0.0) - jnp.log1p(jnp.exp(-jnp.abs(fl)))
    lf_ref[...] = lf

    @pl.when(t % tiles_per_seq == 0)
    def _():
        carry_ref[...] = jnp.zeros_like(carry_ref)

    c = _dot_left_exact(tri_ref[...], lf) + carry_ref[0:1, :]
    c_ref[...] = c
    carry_ref[...] = jnp.broadcast_to(c[tm - 1:tm, :], carry_ref.shape)


def _inproj(x, g, w1, bfl, gains, rope, seg, segt, *, tm, tiles_per_seq):
    m, d_model = x.shape
    n_w = w1.shape[1]
    tri = jnp.asarray(np.tril(np.ones((tm, tm), np.float32)), BF16)
    rope_tiles = rope.shape[0] // tm
    row = lambda n: pl.BlockSpec((tm, n), lambda t: (t, 0))
    outs = [jax.ShapeDtypeStruct((m, W_ATT), F32)] * 6 + [jax.ShapeDtypeStruct((m, LANES), F32)] * 2 \
        + [jax.ShapeDtypeStruct((m, d_model), F32)] * 2
    return pl.pallas_call(
        functools.partial(_inproj_body, tiles_per_seq=tiles_per_seq),
        grid=(m // tm,),
        in_specs=[row(d_model), _const_spec((1, d_model)), _const_spec((d_model, n_w)),
                  _const_spec((1, LANES)), _const_spec((SUBLANES, W_ATT)),
                  pl.BlockSpec((tm, 3 * LANES), lambda t: (t % rope_tiles, 0)),
                  _const_spec((W_ATT, LANES)), _const_spec((LANES, W_ATT)), _const_spec((tm, tm))],
        out_specs=[row(W_ATT)] * 6 + [row(LANES)] * 2 + [row(d_model)] * 2,
        out_shape=outs,
        scratch_shapes=[pltpu.VMEM((SUBLANES, LANES), F32)],
        compiler_params=_params(("arbitrary",)),
        name="inproj",
    )(x, g, w1, bfl, gains, rope, seg, segt, tri)


def _softmax_tile(s, carry, v_t):
    m, l, acc = carry
    m_new = jnp.maximum(m, jnp.max(s, axis=0, keepdims=True))
    alpha = jnp.exp(m - m_new)
    p = jnp.exp(s - m_new)
    l = alpha * l + jnp.sum(p, axis=0, keepdims=True)
    acc = alpha * acc + _dot(v_t, _bf(p))
    return m_new, l, acc


def _softmax_init():
    return (jnp.full((1, ATT_TILE), NEG, F32), jnp.zeros((1, ATT_TILE), F32),
            jnp.zeros((HEAD_DIM, ATT_TILE), F32))


def _head_rows(hh):
    row = lax.broadcasted_iota(jnp.int32, (LANES, 1), 0)
    return (row >= hh * HEAD_DIM) & (row < (hh + 1) * HEAD_DIM)


def _head_lanes(hh):
    lane = lax.broadcasted_iota(jnp.int32, (1, LANES), 1)
    return (lane >= hh * HEAD_DIM) & (lane < (hh + 1) * HEAD_DIM)


def _causal_mask():
    kpos = lax.broadcasted_iota(jnp.int32, (ATT_TILE, ATT_TILE), 0)
    qpos = lax.broadcasted_iota(jnp.int32, (ATT_TILE, ATT_TILE), 1)
    return kpos <= qpos


def _prep_qkv(q_ref, k_ref, v_ref, qtm_ref, vt_ref, kb_ref, nb):
    kb_ref[...] = _bf(k_ref[0])
    for j in range(nb):
        rows = slice(j * ATT_TILE, (j + 1) * ATT_TILE)
        qt = (q_ref[0, rows, :] * SCALE).T
        for hh in range(2):
            qtm_ref[hh, j] = _bf(jnp.where(_head_rows(hh), qt, 0.0))
        vt_ref[j] = _bf(v_ref[0, rows, :].T)


def _fox_prompt_body(q_ref, k_ref, v_ref, ct_ref, c_ref, o_ref, qtm_ref, vt_ref, kb_ref, ckb_ref, *, nb):
    hp = pl.program_id(1)
    i = pl.program_id(2)
    t_len = k_ref.shape[1]

    @pl.when(i == 0)
    def _():
        _prep_qkv(q_ref, k_ref, v_ref, qtm_ref, vt_ref, kb_ref, nb)
        c = c_ref[0]
        lane = lax.broadcasted_iota(jnp.int32, (1, LANES), 1)
        for hh in range(2):
            ck = jnp.sum(jnp.where(lane == 2 * hp + hh, c, 0.0), axis=1, keepdims=True)
            ckb_ref[hh] = jnp.broadcast_to(ck, (t_len, LANES))

    causal = _causal_mask()
    outs = []
    for hh in range(2):
        qt = qtm_ref[hh, i]
        cq = ct_ref[0, 0, pl.ds(2 * hp + hh, 1), :]

        def tile(n, carry, diag, hh=hh, qt=qt, cq=cq):
            rows = pl.ds(pl.multiple_of(n * ATT_TILE, ATT_TILE), ATT_TILE)
            ckn = ckb_ref[hh, rows, :]
            s = _dot(kb_ref[rows, :], qt) + cq - jnp.concatenate([ckn] * (ATT_TILE // LANES), axis=1)
            if diag:
                s = jnp.where(causal, s, NEG)
            return _softmax_tile(s, carry, vt_ref[n, hh * HEAD_DIM:(hh + 1) * HEAD_DIM, :])

        carry = tile(i, _softmax_init(), True)
        _, l, acc = lax.fori_loop(0, i, lambda n, c: tile(n, c, False), carry)
        outs.append(acc / l)
    o_ref[0] = jnp.concatenate(outs, axis=0).T


def _moba_prompt_body(q_ref, k_ref, v_ref, o_ref, qtm_ref, vt_ref, kb_ref, sel_ref, *, nb):
    i = pl.program_id(2)

    @pl.when(i == 0)
    def _():
        _prep_qkv(q_ref, k_ref, v_ref, qtm_ref, vt_ref, kb_ref, nb)
        blk = lax.broadcasted_iota(jnp.int32, (nb, 1), 0)
        km = jnp.zeros((nb, LANES), F32)
        for n in range(nb):
            ksum = jnp.sum(k_ref[0, n * MOBA_BLOCK:(n + 1) * MOBA_BLOCK, :], axis=0, keepdims=True)
            km = jnp.where(blk == n, ksum * (1.0 / MOBA_BLOCK), km)
        for j in range(nb):
            qt_hi, qt_lo = _split2(q_ref[0, j * ATT_TILE:(j + 1) * ATT_TILE, :].T)
            for hh in range(2):
                km_hi, km_lo = _split2(jnp.where(_head_lanes(hh), km, 0.0))
                s = _dot(km_hi, qt_hi) + _dot(km_hi, qt_lo) + _dot(km_lo, qt_hi)
                sm = jnp.where(blk < j, s, -jnp.inf)
                sel = jnp.zeros((nb, ATT_TILE), F32)
                for n in range(j):
                    sn = sm[n:n + 1, :]
                    beats = (sm > sn) | ((sm == sn) & (blk < n))
                    rank = jnp.sum(beats.astype(F32), axis=0, keepdims=True)
                    ok = (rank < MOBA_TOPK) & (jnp.abs(sn) < jnp.inf)
                    sel = jnp.where(blk == n, ok.astype(F32), sel)
                sel_ref[hh, j] = sel

    causal = _causal_mask()
    outs = []
    for hh in range(2):
        qt = qtm_ref[hh, i]

        def tile(n, carry, diag, hh=hh, qt=qt):
            rows = pl.ds(pl.multiple_of(n * ATT_TILE, ATT_TILE), ATT_TILE)
            s = _dot(kb_ref[rows, :], qt)
            if diag:
                s = jnp.where(causal, s, NEG)
            else:
                s = jnp.where(sel_ref[hh, i, pl.ds(n, 1), :] > 0.5, s, NEG)
            return _softmax_tile(s, carry, vt_ref[n, hh * HEAD_DIM:(hh + 1) * HEAD_DIM, :])

        carry = tile(i, _softmax_init(), True)
        _, l, acc = lax.fori_loop(0, i, lambda n, c: tile(n, c, False), carry)
        outs.append(acc / l)
    o_ref[0] = jnp.concatenate(outs, axis=0).T


def _prompt_attention(q, k, v, c=None, c_t=None):
    b, t_len, _ = q.shape
    nb = t_len // ATT_TILE
    n_hp = W_ATT // LANES
    qkv_spec = pl.BlockSpec((1, t_len, LANES), lambda bi, hp, i: (bi, 0, hp))
    scratch = [pltpu.VMEM((2, nb, LANES, ATT_TILE), BF16), pltpu.VMEM((nb, LANES, ATT_TILE), BF16),
               pltpu.VMEM((t_len, LANES), BF16)]
    if c is None:
        body = functools.partial(_moba_prompt_body, nb=nb)
        in_specs = [qkv_spec] * 3
        args = (q, k, v)
        scratch.append(pltpu.VMEM((2, nb, nb, ATT_TILE), F32))
        name = "moba_prompt"
    else:
        body = functools.partial(_fox_prompt_body, nb=nb)
        in_specs = [qkv_spec] * 3 + [
            pl.BlockSpec((1, 1, N_HEADS, ATT_TILE), lambda bi, hp, i: (bi, i, 0, 0)),
            pl.BlockSpec((1, t_len, LANES), lambda bi, hp, i: (bi, 0, 0))]
        args = (q, k, v, c_t, c)
        scratch.append(pltpu.VMEM((2, t_len, LANES), F32))
        name = "fox_prompt"
    return pl.pallas_call(
        body,
        grid=(b, n_hp, nb),
        in_specs=in_specs,
        out_specs=pl.BlockSpec((1, ATT_TILE, LANES), lambda bi, hp, i: (bi, i, hp)),
        out_shape=jax.ShapeDtypeStruct((b, t_len, W_ATT), F32),
        scratch_shapes=scratch,
        compiler_params=_params(("arbitrary", "arbitrary", "arbitrary")),
        name=name,
    )(*args)


def _merge_body(x_ref, ya_ref, yb_ref, sga_ref, sgb_ref, wbm_ref, wbf_ref, wo_ref, o_ref):
    merged = (sga_ref[...] * _dot(_bf(ya_ref[...]), wbm_ref[...])
              + sgb_ref[...] * _dot(_bf(yb_ref[...]), wbf_ref[...]))
    o_ref[...] = x_ref[...] + _dot(_bf(merged), wo_ref[...])


def _merge(x, ya, yb, sga, sgb, wbm, wbf, wo, *, tm):
    m, d_model = x.shape
    row = lambda n: pl.BlockSpec((tm, n), lambda t: (t, 0))
    return pl.pallas_call(
        _merge_body,
        grid=(m // tm,),
        in_specs=[row(d_model), row(W_ATT), row(W_ATT), row(d_model), row(d_model),
                  _const_spec(wbm.shape), _const_spec(wbf.shape), _const_spec(wo.shape)],
        out_specs=row(d_model),
        out_shape=jax.ShapeDtypeStruct((m, d_model), F32),
        compiler_params=_params(("arbitrary",)),
        name="merge",
    )(x, ya, yb, sga, sgb, wbm, wbf, wo)


def _ffn_body(*refs, decode, ck):
    if decode:
        (x_ref, p_ref, buf0_ref, buf1_ref, g_ref, wup_ref, wgate_ref, cw_ref, cb_ref, wdown_ref,
         gp_ref, wpg_ref, wple_ref, o_ref, tail_ref) = refs
    else:
        (x_ref, p_ref, g_ref, wup_ref, wgate_ref, cw_ref, cb_ref, wdown_ref,
         gp_ref, wpg_ref, wple_ref, o_ref, tail_ref, carry_ref) = refs

        @pl.when(pl.program_id(1) == 0)
        def _():
            carry_ref[...] = jnp.zeros_like(carry_ref)

    x = x_ref[...]
    tm = x.shape[0]
    d_ff = wup_ref.shape[1]
    h = _bf(_rms(x) * g_ref[...])
    acc = jnp.zeros_like(x)
    for c in range(d_ff // ck):
        cols = slice(c * ck, (c + 1) * ck)
        u = _dot(h, wup_ref[:, cols])
        a = _dot(h, wgate_ref[:, cols])
        if decode:
            prev2 = buf0_ref[:, cols]
            prev1 = buf1_ref[:, cols]
            tail_ref[:, cols] = a
        else:
            ext = jnp.concatenate([carry_ref[:, cols], a], axis=0)
            prev1 = pltpu.roll(ext, 1, 0)[SUBLANES:, :]
            prev2 = pltpu.roll(ext, 2, 0)[SUBLANES:, :]
            carry_ref[:, cols] = a[tm - SUBLANES:, :]
            tail_ref[0, :, cols] = a[tm - SUBLANES:, :]
        conv = cb_ref[:, cols] + cw_ref[0:1, cols] * prev2 + cw_ref[1:2, cols] * prev1 + cw_ref[2:3, cols] * a
        act = conv * _sigmoid(conv) * u
        acc = acc + _dot(_bf(act), wdown_ref[cols, :])
    x = x + acc
    gate = _sigmoid(_dot(_bf(_rms(x) * gp_ref[...]), wpg_ref[...]))
    o_ref[...] = x + _dot(_bf(p_ref[...]), wple_ref[...]) * gate


def _ffn(x, p, bufs, g, wup, wgate, cw, cb, wdown, gp, wpg, wple, *, n_seq, tm, ck):
    m, d_model = x.shape
    d_ff = wup.shape[1]
    ple = p.shape[1]
    decode = bufs is not None
    weights = (g, wup, wgate, cw, cb, wdown, gp, wpg, wple)
    w_specs = [_const_spec(w.shape) for w in weights]
    if decode:
        row = lambda n: pl.BlockSpec((tm, n), lambda t: (t, 0))
        grid = (m // tm,)
        in_specs = [row(d_model), row(ple), row(d_ff), row(d_ff)] + w_specs
        args = (x, p) + tuple(bufs) + weights
        out_specs = [row(d_model), row(d_ff)]
        out_shape = [jax.ShapeDtypeStruct((m, d_model), F32), jax.ShapeDtypeStruct((m, d_ff), F32)]
        scratch = []
        sem = ("arbitrary",)
    else:
        tps = m // n_seq // tm
        row = lambda n: pl.BlockSpec((tm, n), lambda s, t: (s * tps + t, 0))
        grid = (n_seq, tps)
        in_specs = [row(d_model), row(ple)] + w_specs
        args = (x, p) + weights
        out_specs = [row(d_model), pl.BlockSpec((1, SUBLANES, d_ff), lambda s, t: (s, 0, 0))]
        out_shape = [jax.ShapeDtypeStruct((m, d_model), F32),
                     jax.ShapeDtypeStruct((n_seq, SUBLANES, d_ff), F32)]
        scratch = [pltpu.VMEM((SUBLANES, d_ff), F32)]
        sem = ("arbitrary", "arbitrary")
    return pl.pallas_call(
        functools.partial(_ffn_body, decode=decode, ck=ck),
        grid=grid, in_specs=in_specs, out_specs=out_specs, out_shape=out_shape,
        scratch_shapes=scratch, compiler_params=_params(sem),
        name="ffn_decode" if decode else "ffn_prompt",
    )(*args)


def _moba_score_body(pt_ref, *refs, pg):
    del pt_ref
    k_refs = refs[:pg]
    q_ref, top_ref, sc_ref = refs[pg:]
    j = pl.program_id(1)
    bps = pg // 2
    lane = lax.broadcasted_iota(jnp.int32, (1, LANES), 1)

    @pl.when(j == 0)
    def _():
        sc_ref[...] = jnp.full_like(sc_ref, -jnp.inf)

    qb = jnp.broadcast_to(q_ref[0], (W_ATT, PAGE_SIZE))
    sc = sc_ref[...]
    for m in range(bps):
        t = (k_refs[2 * m][0] + k_refs[2 * m + 1][0]) * qb
        per_pos = jnp.sum(t.reshape(N_HEADS, HEAD_DIM, PAGE_SIZE), axis=1)
        score = jnp.sum(per_pos, axis=1, keepdims=True) * (1.0 / MOBA_BLOCK)
        sc = jnp.where(lane == j * bps + m, score, sc)
    sc_ref[...] = sc

    @pl.when(j == pl.num_programs(1) - 1)
    def _():
        s = sc
        idx = lane.astype(F32)
        top = jnp.zeros((N_HEADS, LANES), F32)
        for r in range(MOBA_TOPK):
            best = jnp.max(s, axis=1, keepdims=True)
            arg = jnp.min(jnp.where(s == best, idx, float(LANES)), axis=1, keepdims=True)
            top = jnp.where(lane == r, arg, top)
            s = jnp.where(idx == arg, -jnp.inf, s)
        top_ref[0] = top.astype(jnp.int32)


def _moba_scores(page_table, pool_k, q_col, *, pg):
    db, n_pages = page_table.shape
    assert MOBA_TOPK <= n_pages // 2 <= LANES
    page_spec = lambda i: pl.BlockSpec((1, W_ATT, PAGE_SIZE), lambda b, j, pt: (pt[b, j * pg + i], 0, 0))
    grid_spec = pltpu.PrefetchScalarGridSpec(
        num_scalar_prefetch=1,
        grid=(db, n_pages // pg),
        in_specs=[page_spec(i) for i in range(pg)] + [
            pl.BlockSpec((1, W_ATT, 1), lambda b, j, pt: (b, 0, 0))],
        out_specs=pl.BlockSpec((1, N_HEADS, LANES), lambda b, j, pt: (b, 0, 0)),
        scratch_shapes=[pltpu.VMEM((N_HEADS, LANES), F32)],
    )
    return pl.pallas_call(
        functools.partial(_moba_score_body, pg=pg),
        grid_spec=grid_spec,
        out_shape=jax.ShapeDtypeStruct((db, N_HEADS, LANES), jnp.int32),
        compiler_params=_params(("arbitrary", "arbitrary")),
        name="moba_scores",
    )(page_table, *([pool_k] * pg), q_col)


def _moba_decode_body(pt_ref, top_ref, *refs, n_pg):
    del pt_ref, top_ref
    k_refs = refs[:n_pg]
    v_refs = refs[n_pg:2 * n_pg]
    q_ref, kn_ref, vn_ref, o_ref = refs[2 * n_pg:]
    q = q_ref[0] * SCALE
    logits = [jnp.sum(k_refs[i][0] * q, axis=0, keepdims=True) for i in range(n_pg)]
    l_new = jnp.sum(q * kn_ref[0], axis=0, keepdims=True)
    m = l_new
    for s in logits:
        m = jnp.maximum(m, jnp.max(s, axis=1, keepdims=True))
    p_new = jnp.exp(l_new - m)
    denom = p_new
    acc = p_new * vn_ref[0]
    for i, s in enumerate(logits):
        p = jnp.exp(s - m)
        denom = denom + jnp.sum(p, axis=1, keepdims=True)
        acc = acc + jnp.sum(v_refs[i][0] * p, axis=1, keepdims=True)
    o_ref[0] = acc / denom


def _moba_decode(page_table, top_flat, pool_k, pool_v, q_col, k_col, v_col):
    db = page_table.shape[0]
    n_pg = 2 * MOBA_TOPK

    def page_spec(r, e):
        def imap(b, h, pt, top):
            blk = top[(b * N_HEADS + h) * MOBA_TOPK + r]
            return (pt[b, 2 * blk + e], h, 0)
        return pl.BlockSpec((1, HEAD_DIM, PAGE_SIZE), imap)

    page_specs = [page_spec(r, e) for r in range(MOBA_TOPK) for e in range(2)]
    vec_spec = pl.BlockSpec((1, HEAD_DIM, 1), lambda b, h, pt, top: (b, h, 0))
    grid_spec = pltpu.PrefetchScalarGridSpec(
        num_scalar_prefetch=2,
        grid=(db, N_HEADS),
        in_specs=page_specs + page_specs + [vec_spec] * 3,
        out_specs=vec_spec,
    )
    return pl.pallas_call(
        functools.partial(_moba_decode_body, n_pg=n_pg),
        grid_spec=grid_spec,
        out_shape=jax.ShapeDtypeStruct((db, W_ATT, 1), F32),
        compiler_params=_params(("arbitrary", "arbitrary")),
        name="moba_decode",
    )(page_table, top_flat, *([pool_k] * n_pg), *([pool_v] * n_pg), q_col, k_col, v_col)


def _fox_decode_body(pt_ref, *refs, pg):
    del pt_ref
    k_refs = refs[:pg]
    v_refs = refs[pg:2 * pg]
    lf_refs = refs[2 * pg:3 * pg]
    q_ref, kn_ref, vn_ref, cn_ref, u_ref, o_ref, m_ref, l_ref, acc_ref, sfx_ref = refs[3 * pg:]
    j = pl.program_id(1)

    @pl.when(j == 0)
    def _():
        m_ref[...] = jnp.full_like(m_ref, NEG)
        l_ref[...] = jnp.zeros_like(l_ref)
        acc_ref[...] = jnp.zeros_like(acc_ref)
        sfx_ref[...] = jnp.zeros_like(sfx_ref)

    head = lax.broadcasted_iota(jnp.int32, (N_HEADS, 1), 0)
    lane = lax.broadcasted_iota(jnp.int32, (1, W_ATT), 1)
    own = (lane >= head * HEAD_DIM) & (lane < (head + 1) * HEAD_DIM)
    q8 = jnp.where(own, q_ref[0] * SCALE, 0.0)
    q8b = _bf(q8)
    cn = cn_ref[0]
    u = u_ref[...]
    m = m_ref[:, 0:1]
    l = l_ref[:, 0:1]
    acc = acc_ref[...]
    sfx = sfx_ref[...]
    for i in range(pg):
        lft = lf_refs[i][0]
        r = _dot_right_exact(lft, u) + sfx
        s = _dot(q8b, _bf(k_refs[i][0])) + cn + r
        m_new = jnp.maximum(m, jnp.max(s, axis=1, keepdims=True))
        alpha = jnp.exp(m - m_new)
        p = jnp.exp(s - m_new)
        l = alpha * l + jnp.sum(p, axis=1, keepdims=True)
        acc = alpha * acc + _dot_nt(_bf(p), _bf(v_refs[i][0]))
        m = m_new
        sfx = sfx + jnp.sum(lft, axis=1, keepdims=True)
    m_ref[...] = jnp.broadcast_to(m, m_ref.shape)
    l_ref[...] = jnp.broadcast_to(l, l_ref.shape)
    acc_ref[...] = acc
    sfx_ref[...] = sfx

    @pl.when(j == pl.num_programs(1) - 1)
    def _():
        cn1 = cn[:, 0:1]
        l_new = jnp.sum(q8 * kn_ref[0], axis=1, keepdims=True) + cn1 - cn1
        m_fin = jnp.maximum(m, l_new)
        a = jnp.exp(m - m_fin)
        p_new = jnp.exp(l_new - m_fin)
        o8 = (a * acc + p_new * vn_ref[0]) / (a * l + p_new)
        o_ref[0] = jnp.sum(jnp.where(own, o8, 0.0), axis=0, keepdims=True)


def _fox_decode(page_table, pool_k, pool_v, pool_lft, q, k_new, v_new, cn, *, pg):
    db, n_pages = page_table.shape

    def pidx(b, j, pt, i):
        return pt[b, n_pages - 1 - (j * pg + i)]

    kv_spec = lambda i: pl.BlockSpec((1, W_ATT, PAGE_SIZE), lambda b, j, pt: (pidx(b, j, pt, i), 0, 0))
    lf_spec = lambda i: pl.BlockSpec((1, N_HEADS, PAGE_SIZE), lambda b, j, pt: (pidx(b, j, pt, i), 0, 0))
    vec_spec = pl.BlockSpec((1, 1, W_ATT), lambda b, j, pt: (b, 0, 0))
    u = jnp.asarray(np.tril(np.ones((PAGE_SIZE, PAGE_SIZE), np.float32), -1), BF16)
    grid_spec = pltpu.PrefetchScalarGridSpec(
        num_scalar_prefetch=1,
        grid=(db, n_pages // pg),
        in_specs=[kv_spec(i) for i in range(pg)] * 2 + [lf_spec(i) for i in range(pg)] + [
            vec_spec, vec_spec, vec_spec,
            pl.BlockSpec((1, N_HEADS, LANES), lambda b, j, pt: (b, 0, 0)),
            pl.BlockSpec((PAGE_SIZE, PAGE_SIZE), lambda b, j, pt: (0, 0))],
        out_specs=vec_spec,
        scratch_shapes=[pltpu.VMEM((N_HEADS, LANES), F32), pltpu.VMEM((N_HEADS, LANES), F32),
                        pltpu.VMEM((N_HEADS, W_ATT), F32), pltpu.VMEM((N_HEADS, LANES), F32)],
    )
    return pl.pallas_call(
        functools.partial(_fox_decode_body, pg=pg),
        grid_spec=grid_spec,
        out_shape=jax.ShapeDtypeStruct((db, 1, W_ATT), F32),
        compiler_params=_params(("arbitrary", "arbitrary")),
        name="fox_decode",
    )(page_table, *([pool_k] * pg), *([pool_v] * pg), *([pool_lft] * pg), q, k_new, v_new, cn, u)


def _rope_table(pos):
    half = ROPE_DIMS // 2
    inv = ROPE_THETA ** (-jnp.arange(0, ROPE_DIMS, 2, dtype=F32) / ROPE_DIMS)
    ang = pos.astype(F32)[:, None] * inv[None, :]
    d = np.arange(LANES) % HEAD_DIM
    sel = jnp.asarray(d % half)
    cos = jnp.where(jnp.asarray(d < ROPE_DIMS)[None, :], jnp.cos(ang)[:, sel], 1.0)
    sin = jnp.sin(ang)[:, sel]
    sin_up = jnp.where(jnp.asarray(d < half)[None, :], -sin, 0.0)
    sin_dn = jnp.where(jnp.asarray((d >= half) & (d < ROPE_DIMS))[None, :], sin, 0.0)
    return jnp.concatenate([cos, sin_up, sin_dn], axis=1)


def kernel(x_prompt, x_sample, cache_moba_k, cache_moba_v, cache_fox_k, cache_fox_v, cache_fox_logf,
           state_ffn_conv, page_table, p_prompt, p_sample, norm_attn_g, w_in, b_forget,
           qnorm_moba, knorm_moba, qnorm_fox, knorm_fox, w_branch_moba, w_branch_fox, w_out,
           norm_ffn_g, w_up, w_gate, conv_w, conv_b, w_down, norm_ple_g, w_ple, w_ple_gate):
    b, t_len, d_model = x_prompt.shape
    db, ts, _ = x_sample.shape
    depth = w_in.shape[0]
    n_pages = page_table.shape[1]
    n_pool = cache_moba_k.shape[1]
    assert depth == 1 and ts == 1 and t_len % ATT_TILE == 0 and n_pages % 2 == 0
    past_len = n_pages * PAGE_SIZE
    d_ff = w_up.shape[2]
    n_att = 6 * W_ATT

    w_in0 = w_in[0]
    w_fl = jnp.pad(w_in0[:, n_att:n_att + N_HEADS], ((0, 0), (0, LANES - N_HEADS)))
    w1 = _bf(jnp.concatenate([w_in0[:, :n_att], w_in0[:, n_att + N_HEADS:], w_fl], axis=1))
    bfl = jnp.pad(b_forget[0], (0, LANES - N_HEADS))[None, :]
    gains = jnp.stack([jnp.tile(g[0], N_HEADS) for g in (qnorm_moba, knorm_moba, qnorm_fox, knorm_fox)])
    gains = jnp.pad(gains, ((0, SUBLANES - 4), (0, 0)))
    seg_np = (np.arange(W_ATT)[:, None] // HEAD_DIM == np.arange(LANES)[None, :]).astype(np.float32)
    seg = jnp.asarray(seg_np, BF16)
    segt = jnp.asarray(seg_np.T, BF16)
    g_attn = norm_attn_g[0][None, :]
    wbm, wbf, wo = _bf(w_branch_moba[0]), _bf(w_branch_fox[0]), _bf(w_out[0])
    ffn_w = (norm_ffn_g[0][None, :], _bf(w_up[0]), _bf(w_gate[0]),
             jnp.pad(conv_w[0], ((0, SUBLANES - CONV_W), (0, 0))), conv_b[0][None, :], _bf(w_down[0]),
             norm_ple_g[0][None, :], _bf(w_ple_gate[0]), _bf(w_ple[0]))

    tm = 256
    xp = x_prompt.reshape(b * t_len, d_model)
    rope_p = _rope_table(jnp.arange(t_len, dtype=jnp.int32))
    qa, ka, va, qb, kb, vb, lf, c, sga, sgb = _inproj(
        xp, g_attn, w1, bfl, gains, rope_p, seg, segt, tm=tm, tiles_per_seq=t_len // tm)
    r3 = lambda a: a.reshape(b, t_len, a.shape[-1])
    ya = _prompt_attention(r3(qa), r3(ka), r3(va))
    c3 = r3(c)
    c_t = c3[:, :, :N_HEADS].reshape(b, t_len // ATT_TILE, ATT_TILE, N_HEADS).transpose(0, 1, 3, 2)
    yb = _prompt_attention(r3(qb), r3(kb), r3(vb), c3, c_t)
    x1 = _merge(xp, ya.reshape(b * t_len, W_ATT), yb.reshape(b * t_len, W_ATT), sga, sgb, wbm, wbf, wo, tm=tm)
    y_p, tail_p = _ffn(x1, p_prompt[0].reshape(b * t_len, -1), None, *ffn_w, n_seq=b, tm=tm, ck=512)

    heads = lambda a, n: a.reshape(1, n, -1, N_HEADS, HEAD_DIM)
    outs_p = (heads(ka, b), heads(va, b), heads(kb, b), heads(vb, b),
              lf[:, :N_HEADS].reshape(1, b, t_len, N_HEADS), tail_p[None, :, SUBLANES - (CONV_W - 1):, :])

    xs = x_sample.reshape(db, d_model)
    rope_s = _rope_table(jnp.full((db,), past_len, jnp.int32))
    qa, ka, va, qb, kb, vb, lf, _, sga, sgb = _inproj(
        xs, g_attn, w1, bfl, gains, rope_s, seg, segt, tm=db, tiles_per_seq=1)
    v3 = lambda a: a.reshape(db, 1, W_ATT)
    col = lambda a: a.reshape(db, W_ATT, 1)
    pool = lambda cache: cache[0].transpose(0, 2, 3, 1).reshape(n_pool, W_ATT, PAGE_SIZE)
    pool_mk = pool(cache_moba_k)
    top = _moba_scores(page_table, pool_mk, col(qa), pg=min(16, n_pages))
    top_flat = top[:, :, :MOBA_TOPK].reshape(-1)
    ya = _moba_decode(page_table, top_flat, pool_mk, pool(cache_moba_v), col(qa), col(ka), col(va))
    cn = jnp.broadcast_to(lf[:, :N_HEADS, None], (db, N_HEADS, LANES))
    yb = _fox_decode(page_table, pool(cache_fox_k), pool(cache_fox_v),
                     cache_fox_logf[0].transpose(0, 2, 1), v3(qb), v3(kb), v3(vb), cn, pg=min(8, n_pages))
    x1 = _merge(xs, ya.reshape(db, W_ATT), yb.reshape(db, W_ATT), sga, sgb, wbm, wbf, wo, tm=db)
    bufs = (state_ffn_conv[0, :, 0, :], state_ffn_conv[0, :, 1, :])
    y_s, a_s = _ffn(x1, p_sample[0].reshape(db, -1), bufs, *ffn_w, n_seq=db, tm=db, ck=512)
    conv_s = jnp.stack([state_ffn_conv[0, :, 1, :], a_s], axis=1)[None]
    outs_s = (heads(ka, db), heads(va, db), heads(kb, db), heads(vb, db),
              lf[:, :N_HEADS].reshape(1, db, ts, N_HEADS), conv_s)

    return (y_p.reshape(b, t_len, d_model), y_s.reshape(db, ts, d_model)) + outs_p + outs_s
```

```python
import functools

import jax
import jax.numpy as jnp
import numpy as np
from jax import lax
from jax.experimental import pallas as pl
from jax.experimental.pallas import tpu as pltpu

F32 = jnp.float32
BF16 = jnp.bfloat16

HEAD_DIM = 64
N_HEADS = 8
W_ATT = N_HEADS * HEAD_DIM
MOBA_BLOCK = 256
MOBA_TOPK = 3
PAGE_SIZE = 128
ROPE_THETA = 500000.0
ROPE_DIMS = HEAD_DIM // 4
CONV_W = 3
EPS = 1e-6
SCALE = HEAD_DIM ** -0.5
NEG = -1e30

LANES = 128
SUBLANES = 8
BLK_PER_TILE = 2
ATT_TILE = BLK_PER_TILE * MOBA_BLOCK
VMEM_LIMIT = 56 * 1024 * 1024


def _bf(x):
    return x.astype(BF16)


def _dot(a, b):
    return jnp.dot(a, b, preferred_element_type=F32)


def _dot_nt(a, b):
    return lax.dot_general(a, b, (((1,), (1,)), ((), ())), preferred_element_type=F32)


def _split2(x):
    hi = _bf(x)
    lo = _bf(x - hi.astype(F32))
    return hi, lo


def _split3(x):
    hi = _bf(x)
    r = x - hi.astype(F32)
    mid = _bf(r)
    lo = _bf(r - mid.astype(F32))
    return hi, mid, lo


def _dot_left_exact(a_bf, x):
    hi, mid, lo = _split3(x)
    return _dot(a_bf, hi) + _dot(a_bf, mid) + _dot(a_bf, lo)


def _dot_right_exact(x, b_bf):
    hi, mid, lo = _split3(x)
    return _dot(hi, b_bf) + _dot(mid, b_bf) + _dot(lo, b_bf)


def _sigmoid(x):
    return 1.0 / (1.0 + jnp.exp(-x))


def _rms(x):
    return x * lax.rsqrt(jnp.mean(x * x, axis=-1, keepdims=True) + EPS)


def _const_spec(shape):
    nd = len(shape)
    return pl.BlockSpec(shape, lambda *_: (0,) * nd, pipeline_mode=pl.Buffered(1))


def _params(sem):
    return pltpu.CompilerParams(dimension_semantics=sem, vmem_limit_bytes=VMEM_LIMIT)


def _inproj_body(x_ref, g_ref, w_ref, bfl_ref, gains_ref, rope_ref, seg_ref, segt_ref, tri_ref,
                 qa_ref, ka_ref, va_ref, qb_ref, kb_ref, vb_ref, lf_ref, c_ref, sga_ref, sgb_ref,
                 carry_ref, *, tiles_per_seq, kv_transposed):
    t = pl.program_id(0)
    tm = x_ref.shape[0]
    d_model = x_ref.shape[1]
    h = _bf(_rms(x_ref[...]) * g_ref[...])
    seg = seg_ref[...]
    segt = segt_ref[...]
    rope = rope_ref[...]

    def widen(tbl):
        return jnp.concatenate([tbl] * (W_ATT // LANES), axis=1)

    cos_t = widen(rope[:, 0:LANES])
    sin_up = widen(rope[:, LANES:2 * LANES])
    sin_dn = widen(rope[:, 2 * LANES:3 * LANES])

    def proj(c0, n):
        return _dot(h, w_ref[:, c0:c0 + n])

    def headnorm(y, gain):
        ss = _dot(_bf(y * y), seg)
        r = lax.rsqrt(ss * (1.0 / HEAD_DIM) + EPS)
        r_hi, r_lo = _split2(r)
        return y * (_dot(r_hi, segt) + _dot(r_lo, segt)) * gain

    def rope_fn(y):
        half = ROPE_DIMS // 2
        return (y * cos_t + pltpu.roll(y, W_ATT - half, 1) * sin_up
                + pltpu.roll(y, half, 1) * sin_dn)

    def put_kv(ref, y):
        if kv_transposed:
            ref[0] = y.T
        else:
            ref[...] = y

    qa_ref[...] = rope_fn(headnorm(proj(0, W_ATT), gains_ref[0:1, :]))
    put_kv(ka_ref, rope_fn(headnorm(proj(W_ATT, W_ATT), gains_ref[1:2, :])))
    put_kv(va_ref, proj(2 * W_ATT, W_ATT))
    qb_ref[...] = headnorm(proj(3 * W_ATT, W_ATT), gains_ref[2:3, :])
    put_kv(kb_ref, headnorm(proj(4 * W_ATT, W_ATT), gains_ref[3:4, :]))
    put_kv(vb_ref, proj(5 * W_ATT, W_ATT))
    sga_ref[...] = _sigmoid(proj(6 * W_ATT, d_model))
    sgb_ref[...] = _sigmoid(proj(6 * W_ATT + d_model, d_model))

    fl = proj(6 * W_ATT + 2 * d_model, LANES) + bfl_ref[...]
    lf = jnp.minimum(fl, 0.0) - jnp.log1p(jnp.exp(-jnp.abs(fl)))
    lf_ref[...] = lf

    @pl.when(t % tiles_per_seq == 0)
    def _():
        carry_ref[...] = jnp.zeros_like(carry_ref)

    c = _dot_left_exact(tri_ref[...], lf) + carry_ref[0:1, :]
    c_ref[...] = c
    carry_ref[...] = jnp.broadcast_to(c[tm - 1:tm, :], carry_ref.shape)


def _inproj(x, g, w1, bfl, gains, rope, seg, segt, *, tm, tiles_per_seq, kv_transposed):
    m, d_model = x.shape
    n_w = w1.shape[1]
    tri = jnp.asarray(np.tril(np.ones((tm, tm), np.float32)), BF16)
    rope_tiles = rope.shape[0] // tm
    row = lambda n: pl.BlockSpec((tm, n), lambda t: (t, 0))
    tok = jax.ShapeDtypeStruct((m, W_ATT), F32)
    if kv_transposed:
        tps = tiles_per_seq
        kv = jax.ShapeDtypeStruct((m // (tps * tm), W_ATT, tps * tm), F32)
        kv_spec = pl.BlockSpec((1, W_ATT, tm), lambda t: (t // tps, 0, t % tps))
    else:
        kv, kv_spec = tok, row(W_ATT)
    outs = [tok, kv, kv, tok, kv, kv] + [jax.ShapeDtypeStruct((m, LANES), F32)] * 2 \
        + [jax.ShapeDtypeStruct((m, d_model), F32)] * 2
    qkv_specs = [row(W_ATT), kv_spec, kv_spec] * 2
    return pl.pallas_call(
        functools.partial(_inproj_body, tiles_per_seq=tiles_per_seq, kv_transposed=kv_transposed),
        grid=(m // tm,),
        in_specs=[row(d_model), _const_spec((1, d_model)), _const_spec((d_model, n_w)),
                  _const_spec((1, LANES)), _const_spec((SUBLANES, W_ATT)),
                  pl.BlockSpec((tm, 3 * LANES), lambda t: (t % rope_tiles, 0)),
                  _const_spec((W_ATT, LANES)), _const_spec((LANES, W_ATT)), _const_spec((tm, tm))],
        out_specs=qkv_specs + [row(LANES)] * 2 + [row(d_model)] * 2,
        out_shape=outs,
        scratch_shapes=[pltpu.VMEM((SUBLANES, LANES), F32)],
        compiler_params=_params(("arbitrary",)),
        name="inproj",
    )(x, g, w1, bfl, gains, rope, seg, segt, tri)


def _softmax_tile(s, carry, v_t):
    m, l, acc = carry
    m_new = jnp.maximum(m, jnp.max(s, axis=0, keepdims=True))
    alpha = jnp.exp(m - m_new)
    p = jnp.exp(s - m_new)
    l = alpha * l + jnp.sum(p, axis=0, keepdims=True)
    acc = alpha * acc + _dot(v_t, _bf(p))
    return m_new, l, acc


def _softmax_init():
    return (jnp.full((1, ATT_TILE), NEG, F32), jnp.zeros((1, ATT_TILE), F32),
            jnp.zeros((HEAD_DIM, ATT_TILE), F32))


def _split3_f32(x):
    hi = _bf(x).astype(F32)
    r = x - hi
    mid = _bf(r).astype(F32)
    return hi, mid, _bf(r - mid).astype(F32)


def _head_rows(hh):
    row = lax.broadcasted_iota(jnp.int32, (LANES, 1), 0)
    return (row >= hh * HEAD_DIM) & (row < (hh + 1) * HEAD_DIM)


def _head_lanes(hh):
    lane = lax.broadcasted_iota(jnp.int32, (1, LANES), 1)
    return (lane >= hh * HEAD_DIM) & (lane < (hh + 1) * HEAD_DIM)


def _causal_mask():
    kpos = lax.broadcasted_iota(jnp.int32, (ATT_TILE, ATT_TILE), 0)
    qpos = lax.broadcasted_iota(jnp.int32, (ATT_TILE, ATT_TILE), 1)
    return kpos <= qpos


def _fox_bias_lanes(c, hp):
    lane = lax.broadcasted_iota(jnp.int32, (1, LANES), 1)
    aug = jnp.zeros(c.shape, F32)
    for hh in range(2):
        ck = jnp.sum(jnp.where(lane == 2 * hp + hh, c, 0.0), axis=1, keepdims=True)
        base = SUBLANES * hh
        for off, part in enumerate(_split3_f32(ck)):
            aug = jnp.where(lane == base + off, -part, aug)
        aug = jnp.where((lane >= base + 3) & (lane < base + 6), 1.0, aug)
    return aug


def _fox_bias_rows(cq, hh):
    row = lax.broadcasted_iota(jnp.int32, (LANES, 1), 0)
    base = SUBLANES * hh
    aug = jnp.where((row >= base) & (row < base + 3), 1.0, jnp.zeros((LANES, cq.shape[1]), F32))
    for off, part in enumerate(_split3_f32(cq)):
        aug = jnp.where(row == base + 3 + off, part, aug)
    return aug


def _moba_select(q_ref, km, sel_ref, ns, nb):
    blk = lax.broadcasted_iota(jnp.int32, (nb, 1), 0)
    qlane = lax.broadcasted_iota(jnp.int32, (1, ATT_TILE), 1)
    for j in range(ns):
        own = jnp.zeros((1, ATT_TILE), jnp.int32) + BLK_PER_TILE * j
        for e in range(1, BLK_PER_TILE):
            own = jnp.where(qlane >= e * MOBA_BLOCK, BLK_PER_TILE * j + e, own)
        past = blk < own
        qt_hi, qt_lo = _split2(q_ref[0, j * ATT_TILE:(j + 1) * ATT_TILE, :].T)
        for hh in range(2):
            km_hi, km_lo = _split2(jnp.where(_head_lanes(hh), km, 0.0))
            s = _dot(km_hi, qt_hi) + _dot(km_hi, qt_lo) + _dot(km_lo, qt_hi)
            sm = jnp.where(past, s, -jnp.inf)
            sel = jnp.zeros((nb, ATT_TILE), F32)
            for n in range(BLK_PER_TILE * (j + 1) - 1):
                sn = sm[n:n + 1, :]
                beats = (sm > sn) | ((sm == sn) & (blk < n))
                rank = jnp.sum(beats.astype(F32), axis=0, keepdims=True)
                ok = (rank < MOBA_TOPK) & (jnp.abs(sn) < jnp.inf)
                sel = jnp.where(blk == n, ok.astype(F32), sel)
            sel_ref[hh, j] = jnp.where(blk == own, 1.0, sel)


def _prompt_attn_body(*refs, ns, nb, fox):
    if fox:
        q_ref, k_ref, v_ref, ct_ref, c_ref, o_ref, qtm_ref, vt_ref, kb_ref = refs
    else:
        q_ref, k_ref, v_ref, o_ref, qtm_ref, vt_ref, kb_ref, sel_ref = refs
    hp = pl.program_id(1)
    i = pl.program_id(2)

    @pl.when(i == 0)
    def _():
        blk = lax.broadcasted_iota(jnp.int32, (nb, 1), 0)
        km = jnp.zeros((nb, LANES), F32)
        for j in range(ns):
            rows = slice(j * ATT_TILE, (j + 1) * ATT_TILE)
            kf = k_ref[0, :, rows].T
            kb_ref[rows, 0:LANES] = _bf(kf)
            vt_ref[j] = _bf(v_ref[0, :, rows])
            if not fox:
                for e in range(BLK_PER_TILE):
                    ksum = jnp.sum(kf[e * MOBA_BLOCK:(e + 1) * MOBA_BLOCK, :], axis=0, keepdims=True)
                    km = jnp.where(blk == BLK_PER_TILE * j + e, ksum * (1.0 / MOBA_BLOCK), km)
        if fox:
            kb_ref[:, LANES:2 * LANES] = _bf(_fox_bias_lanes(c_ref[0], hp))
        else:
            _moba_select(q_ref, km, sel_ref, ns, nb)
        for j in range(ns):
            rows = slice(j * ATT_TILE, (j + 1) * ATT_TILE)
            qt = (q_ref[0, rows, :] * SCALE).T
            for hh in range(2):
                qtm_ref[hh, j, 0:LANES, :] = _bf(jnp.where(_head_rows(hh), qt, 0.0))
                if fox:
                    cq = ct_ref[0, pl.ds(2 * hp + hh, 1), rows]
                    qtm_ref[hh, j, LANES:2 * LANES, :] = _bf(_fox_bias_rows(cq, hh))

    causal = _causal_mask()

    def tile(n, carry, diag):
        rows = pl.ds(pl.multiple_of(n * ATT_TILE, ATT_TILE), ATT_TILE)
        kn = kb_ref[rows, :]
        new = []
        for hh in range(2):
            s = _dot(kn, qtm_ref[hh, i])
            if fox:
                keep = causal if diag else None
            else:
                keep = jnp.concatenate(
                    [jnp.broadcast_to(sel_ref[hh, i, pl.ds(BLK_PER_TILE * n + e, 1), :],
                                      (MOBA_BLOCK, ATT_TILE)) for e in range(BLK_PER_TILE)], axis=0) > 0.5
                if diag:
                    keep = keep & causal
            if keep is not None:
                s = jnp.where(keep, s, NEG)
            new.append(_softmax_tile(s, carry[hh], vt_ref[n, hh * HEAD_DIM:(hh + 1) * HEAD_DIM, :]))
        return tuple(new)

    carry = tile(i, (_softmax_init(), _softmax_init()), True)
    carry = lax.fori_loop(0, i, lambda n, c: tile(n, c, False), carry)
    o_ref[0] = jnp.concatenate([acc / l for _, l, acc in carry], axis=0).T


def _prompt_attention(q, k, v, c=None, c_t=None):
    b, t_len, _ = q.shape
    ns = t_len // ATT_TILE
    nb = t_len // MOBA_BLOCK
    n_hp = W_ATT // LANES
    fox = c is not None
    kdim = 2 * LANES if fox else LANES
    q_spec = pl.BlockSpec((1, t_len, LANES), lambda bi, hp, i: (bi, 0, hp))
    kv_spec = pl.BlockSpec((1, LANES, t_len), lambda bi, hp, i: (bi, hp, 0))
    qkv_specs = [q_spec, kv_spec, kv_spec]
    scratch = [pltpu.VMEM((2, ns, kdim, ATT_TILE), BF16), pltpu.VMEM((ns, LANES, ATT_TILE), BF16),
               pltpu.VMEM((t_len, kdim), BF16)]
    if fox:
        in_specs = qkv_specs + [
            pl.BlockSpec((1, N_HEADS, t_len), lambda bi, hp, i: (bi, 0, 0)),
            pl.BlockSpec((1, t_len, LANES), lambda bi, hp, i: (bi, 0, 0))]
        args = (q, k, v, c_t, c)
    else:
        in_specs = qkv_specs
        args = (q, k, v)
        scratch.append(pltpu.VMEM((2, ns, nb, ATT_TILE), F32))
    return pl.pallas_call(
        functools.partial(_prompt_attn_body, ns=ns, nb=nb, fox=fox),
        grid=(b, n_hp, ns),
        in_specs=in_specs,
        out_specs=pl.BlockSpec((1, ATT_TILE, LANES), lambda bi, hp, i: (bi, i, hp)),
        out_shape=jax.ShapeDtypeStruct((b, t_len, W_ATT), F32),
        scratch_shapes=scratch,
        compiler_params=_params(("arbitrary", "arbitrary", "arbitrary")),
        name="fox_prompt" if fox else "moba_prompt",
    )(*args)


def _merge_body(x_ref, ya_ref, yb_ref, sga_ref, sgb_ref, wbm_ref, wbf_ref, wo_ref, o_ref):
    merged = (sga_ref[...] * _dot(_bf(ya_ref[...]), wbm_ref[...])
              + sgb_ref[...] * _dot(_bf(yb_ref[...]), wbf_ref[...]))
    o_ref[...] = x_ref[...] + _dot(_bf(merged), wo_ref[...])


def _merge(x, ya, yb, sga, sgb, wbm, wbf, wo, *, tm):
    m, d_model = x.shape
    row = lambda n: pl.BlockSpec((tm, n), lambda t: (t, 0))
    return pl.pallas_call(
        _merge_body,
        grid=(m // tm,),
        in_specs=[row(d_model), row(W_ATT), row(W_ATT), row(d_model), row(d_model),
                  _const_spec(wbm.shape), _const_spec(wbf.shape), _const_spec(wo.shape)],
        out_specs=row(d_model),
        out_shape=jax.ShapeDtypeStruct((m, d_model), F32),
        compiler_params=_params(("arbitrary",)),
        name="merge",
    )(x, ya, yb, sga, sgb, wbm, wbf, wo)


def _ffn_body(*refs, decode, ck):
    if decode:
        (x_ref, p_ref, buf0_ref, buf1_ref, g_ref, wup_ref, wgate_ref, cw_ref, cb_ref, wdown_ref,
         gp_ref, wpg_ref, wple_ref, o_ref, tail_ref) = refs
    else:
        (x_ref, p_ref, g_ref, wup_ref, wgate_ref, cw_ref, cb_ref, wdown_ref,
         gp_ref, wpg_ref, wple_ref, o_ref, tail_ref, carry_ref) = refs

        @pl.when(pl.program_id(1) == 0)
        def _():
            carry_ref[...] = jnp.zeros_like(carry_ref)

    x = x_ref[...]
    tm = x.shape[0]
    d_ff = wup_ref.shape[1]
    h = _bf(_rms(x) * g_ref[...])
    acc = jnp.zeros_like(x)
    for c in range(d_ff // ck):
        cols = slice(c * ck, (c + 1) * ck)
        u = _dot(h, wup_ref[:, cols])
        a = _dot(h, wgate_ref[:, cols])
        if decode:
            prev2 = buf0_ref[:, cols]
            prev1 = buf1_ref[:, cols]
            tail_ref[:, cols] = a
        else:
            ext = jnp.concatenate([carry_ref[:, cols], a], axis=0)
            prev1 = pltpu.roll(ext, 1, 0)[SUBLANES:, :]
            prev2 = pltpu.roll(ext, 2, 0)[SUBLANES:, :]
            carry_ref[:, cols] = a[tm - SUBLANES:, :]
            tail_ref[0, :, cols] = a[tm - SUBLANES:, :]
        conv = cb_ref[:, cols] + cw_ref[0:1, cols] * prev2 + cw_ref[1:2, cols] * prev1 + cw_ref[2:3, cols] * a
        act = conv * _sigmoid(conv) * u
        acc = acc + _dot(_bf(act), wdown_ref[cols, :])
    x = x + acc
    gate = _sigmoid(_dot(_bf(_rms(x) * gp_ref[...]), wpg_ref[...]))
    o_ref[...] = x + _dot(_bf(p_ref[...]), wple_ref[...]) * gate


def _ffn(x, p, bufs, g, wup, wgate, cw, cb, wdown, gp, wpg, wple, *, n_seq, tm, ck):
    m, d_model = x.shape
    d_ff = wup.shape[1]
    ple = p.shape[1]
    decode = bufs is not None
    weights = (g, wup, wgate, cw, cb, wdown, gp, wpg, wple)
    w_specs = [_const_spec(w.shape) for w in weights]
    if decode:
        row = lambda n: pl.BlockSpec((tm, n), lambda t: (t, 0))
        grid = (m // tm,)
        in_specs = [row(d_model), row(ple), row(d_ff), row(d_ff)] + w_specs
        args = (x, p) + tuple(bufs) + weights
        out_specs = [row(d_model), row(d_ff)]
        out_shape = [jax.ShapeDtypeStruct((m, d_model), F32), jax.ShapeDtypeStruct((m, d_ff), F32)]
        scratch = []
        sem = ("arbitrary",)
    else:
        tps = m // n_seq // tm
        row = lambda n: pl.BlockSpec((tm, n), lambda s, t: (s * tps + t, 0))
        grid = (n_seq, tps)
        in_specs = [row(d_model), row(ple)] + w_specs
        args = (x, p) + weights
        out_specs = [row(d_model), pl.BlockSpec((1, SUBLANES, d_ff), lambda s, t: (s, 0, 0))]
        out_shape = [jax.ShapeDtypeStruct((m, d_model), F32),
                     jax.ShapeDtypeStruct((n_seq, SUBLANES, d_ff), F32)]
        scratch = [pltpu.VMEM((SUBLANES, d_ff), F32)]
        sem = ("arbitrary", "arbitrary")
    return pl.pallas_call(
        functools.partial(_ffn_body, decode=decode, ck=ck),
        grid=grid, in_specs=in_specs, out_specs=out_specs, out_shape=out_shape,
        scratch_shapes=scratch, compiler_params=_params(sem),
        name="ffn_decode" if decode else "ffn_prompt",
    )(*args)


def _moba_score_body(pt_ref, *refs, pg):
    del pt_ref
    k_refs = refs[:pg]
    q_ref, top_ref, sc_ref = refs[pg:]
    j = pl.program_id(1)
    bps = pg // 2
    lane = lax.broadcasted_iota(jnp.int32, (1, LANES), 1)

    @pl.when(j == 0)
    def _():
        sc_ref[...] = jnp.full_like(sc_ref, -jnp.inf)

    qb = jnp.broadcast_to(q_ref[0], (W_ATT, PAGE_SIZE))
    sc = sc_ref[...]
    for m in range(bps):
        t = (k_refs[2 * m][0] + k_refs[2 * m + 1][0]) * qb
        per_pos = jnp.sum(t.reshape(N_HEADS, HEAD_DIM, PAGE_SIZE), axis=1)
        score = jnp.sum(per_pos, axis=1, keepdims=True) * (1.0 / MOBA_BLOCK)
        sc = jnp.where(lane == j * bps + m, score, sc)
    sc_ref[...] = sc

    @pl.when(j == pl.num_programs(1) - 1)
    def _():
        s = sc
        idx = lane.astype(F32)
        top = jnp.zeros((N_HEADS, LANES), F32)
        for r in range(MOBA_TOPK):
            best = jnp.max(s, axis=1, keepdims=True)
            arg = jnp.min(jnp.where(s == best, idx, float(LANES)), axis=1, keepdims=True)
            top = jnp.where(lane == r, arg, top)
            s = jnp.where(idx == arg, -jnp.inf, s)
        top_ref[0] = top.astype(jnp.int32)


def _moba_scores(page_table, pool_k, q_col, *, pg):
    db, n_pages = page_table.shape
    assert MOBA_TOPK <= n_pages // 2 <= LANES
    page_spec = lambda i: pl.BlockSpec((1, W_ATT, PAGE_SIZE), lambda b, j, pt: (pt[b, j * pg + i], 0, 0))
    grid_spec = pltpu.PrefetchScalarGridSpec(
        num_scalar_prefetch=1,
        grid=(db, n_pages // pg),
        in_specs=[page_spec(i) for i in range(pg)] + [
            pl.BlockSpec((1, W_ATT, 1), lambda b, j, pt: (b, 0, 0))],
        out_specs=pl.BlockSpec((1, N_HEADS, LANES), lambda b, j, pt: (b, 0, 0)),
        scratch_shapes=[pltpu.VMEM((N_HEADS, LANES), F32)],
    )
    return pl.pallas_call(
        functools.partial(_moba_score_body, pg=pg),
        grid_spec=grid_spec,
        out_shape=jax.ShapeDtypeStruct((db, N_HEADS, LANES), jnp.int32),
        compiler_params=_params(("arbitrary", "arbitrary")),
        name="moba_scores",
    )(page_table, *([pool_k] * pg), q_col)


def _moba_decode_body(pt_ref, top_ref, *refs, n_pg, hps):
    del pt_ref, top_ref
    k_refs = refs[:hps * n_pg]
    v_refs = refs[hps * n_pg:2 * hps * n_pg]
    q_ref, kn_ref, vn_ref, o_ref = refs[2 * hps * n_pg:]
    for hl in range(hps):
        dims = slice(hl * HEAD_DIM, (hl + 1) * HEAD_DIM)
        pages = range(hl * n_pg, (hl + 1) * n_pg)
        q = q_ref[0, dims, :] * SCALE
        logits = [jnp.sum(k_refs[i][0] * q, axis=0, keepdims=True) for i in pages]
        l_new = jnp.sum(q * kn_ref[0, dims, :], axis=0, keepdims=True)
        s_max = logits[0]
        for s in logits[1:]:
            s_max = jnp.maximum(s_max, s)
        m = jnp.maximum(l_new, jnp.max(s_max, axis=1, keepdims=True))
        p_new = jnp.exp(l_new - m)
        p_sum = jnp.zeros((1, PAGE_SIZE), F32)
        pv = jnp.zeros((HEAD_DIM, PAGE_SIZE), F32)
        for i, s in zip(pages, logits):
            p = jnp.exp(s - m)
            p_sum = p_sum + p
            pv = pv + v_refs[i][0] * p
        denom = p_new + jnp.sum(p_sum, axis=1, keepdims=True)
        acc = p_new * vn_ref[0, dims, :] + jnp.sum(pv, axis=1, keepdims=True)
        o_ref[0, dims, :] = acc / denom


def _moba_decode(page_table, top_flat, pool_k, pool_v, q_col, k_col, v_col, *, hps):
    db = page_table.shape[0]
    n_pg = 2 * MOBA_TOPK

    def page_spec(hl, r, e):
        def imap(b, g, pt, top):
            h = g * hps + hl
            blk = top[(b * N_HEADS + h) * MOBA_TOPK + r]
            return (pt[b, 2 * blk + e], h, 0)
        return pl.BlockSpec((1, HEAD_DIM, PAGE_SIZE), imap)

    page_specs = [page_spec(hl, r, e) for hl in range(hps) for r in range(MOBA_TOPK) for e in range(2)]
    vec_spec = pl.BlockSpec((1, hps * HEAD_DIM, 1), lambda b, g, pt, top: (b, g, 0))
    grid_spec = pltpu.PrefetchScalarGridSpec(
        num_scalar_prefetch=2,
        grid=(db, N_HEADS // hps),
        in_specs=page_specs + page_specs + [vec_spec] * 3,
        out_specs=vec_spec,
    )
    return pl.pallas_call(
        functools.partial(_moba_decode_body, n_pg=n_pg, hps=hps),
        grid_spec=grid_spec,
        out_shape=jax.ShapeDtypeStruct((db, W_ATT, 1), F32),
        compiler_params=_params(("arbitrary", "arbitrary")),
        name="moba_decode",
    )(page_table, top_flat, *([pool_k] * (hps * n_pg)), *([pool_v] * (hps * n_pg)), q_col, k_col, v_col)


def _fox_decode_body(pt_ref, *refs, pg):
    del pt_ref
    k_refs = refs[:pg]
    v_refs = refs[pg:2 * pg]
    lf_refs = refs[2 * pg:3 * pg]
    q_ref, kn_ref, vn_ref, cn_ref, u_ref, o_ref, m_ref, l_ref, acc_ref, sfx_ref = refs[3 * pg:]
    j = pl.program_id(1)

    @pl.when(j == 0)
    def _():
        m_ref[...] = jnp.full_like(m_ref, NEG)
        l_ref[...] = jnp.zeros_like(l_ref)
        acc_ref[...] = jnp.zeros_like(acc_ref)
        sfx_ref[...] = jnp.zeros_like(sfx_ref)

    head = lax.broadcasted_iota(jnp.int32, (N_HEADS, 1), 0)
    lane = lax.broadcasted_iota(jnp.int32, (1, W_ATT), 1)
    own = (lane >= head * HEAD_DIM) & (lane < (head + 1) * HEAD_DIM)
    q8 = jnp.where(own, q_ref[0] * SCALE, 0.0)
    q8b = _bf(q8)
    cn = cn_ref[0]
    u = u_ref[...]
    m = m_ref[:, 0:1]
    l = l_ref[:, 0:1]
    acc = acc_ref[...]
    sfx = sfx_ref[...]
    lfts = [lf_refs[i][0] for i in range(pg)]
    within = _dot_right_exact(jnp.concatenate(lfts, axis=0), u)
    logits = []
    for i in range(pg):
        r = within[i * N_HEADS:(i + 1) * N_HEADS, :] + sfx
        logits.append(_dot(q8b, _bf(k_refs[i][0])) + cn + r)
        sfx = sfx + jnp.sum(lfts[i], axis=1, keepdims=True)
    s_max = logits[0]
    for s in logits[1:]:
        s_max = jnp.maximum(s_max, s)
    m_new = jnp.maximum(m, jnp.max(s_max, axis=1, keepdims=True))
    alpha = jnp.exp(m - m_new)
    p_sum = jnp.zeros((N_HEADS, PAGE_SIZE), F32)
    pv = jnp.zeros((N_HEADS, W_ATT), F32)
    for i, s in enumerate(logits):
        p = jnp.exp(s - m_new)
        p_sum = p_sum + p
        pv = pv + _dot_nt(_bf(p), _bf(v_refs[i][0]))
    l = alpha * l + jnp.sum(p_sum, axis=1, keepdims=True)
    acc = alpha * acc + pv
    m = m_new
    m_ref[...] = jnp.broadcast_to(m, m_ref.shape)
    l_ref[...] = jnp.broadcast_to(l, l_ref.shape)
    acc_ref[...] = acc
    sfx_ref[...] = sfx

    @pl.when(j == pl.num_programs(1) - 1)
    def _():
        cn1 = cn[:, 0:1]
        l_new = jnp.sum(q8 * kn_ref[0], axis=1, keepdims=True) + cn1 - cn1
        m_fin = jnp.maximum(m, l_new)
        a = jnp.exp(m - m_fin)
        p_new = jnp.exp(l_new - m_fin)
        o8 = (a * acc + p_new * vn_ref[0]) / (a * l + p_new)
        o_ref[0] = jnp.sum(jnp.where(own, o8, 0.0), axis=0, keepdims=True)


def _fox_decode(page_table, pool_k, pool_v, pool_lft, q, k_new, v_new, cn, *, pg):
    db, n_pages = page_table.shape

    def pidx(b, j, pt, i):
        return pt[b, n_pages - 1 - (j * pg + i)]

    kv_spec = lambda i: pl.BlockSpec((1, W_ATT, PAGE_SIZE), lambda b, j, pt: (pidx(b, j, pt, i), 0, 0))
    lf_spec = lambda i: pl.BlockSpec((1, N_HEADS, PAGE_SIZE), lambda b, j, pt: (pidx(b, j, pt, i), 0, 0))
    vec_spec = pl.BlockSpec((1, 1, W_ATT), lambda b, j, pt: (b, 0, 0))
    u = jnp.asarray(np.tril(np.ones((PAGE_SIZE, PAGE_SIZE), np.float32), -1), BF16)
    grid_spec = pltpu.PrefetchScalarGridSpec(
        num_scalar_prefetch=1,
        grid=(db, n_pages // pg),
        in_specs=[kv_spec(i) for i in range(pg)] * 2 + [lf_spec(i) for i in range(pg)] + [
            vec_spec, vec_spec, vec_spec,
            pl.BlockSpec((1, N_HEADS, LANES), lambda b, j, pt: (b, 0, 0)),
            pl.BlockSpec((PAGE_SIZE, PAGE_SIZE), lambda b, j, pt: (0, 0))],
        out_specs=vec_spec,
        scratch_shapes=[pltpu.VMEM((N_HEADS, LANES), F32), pltpu.VMEM((N_HEADS, LANES), F32),
                        pltpu.VMEM((N_HEADS, W_ATT), F32), pltpu.VMEM((N_HEADS, LANES), F32)],
    )
    return pl.pallas_call(
        functools.partial(_fox_decode_body, pg=pg),
        grid_spec=grid_spec,
        out_shape=jax.ShapeDtypeStruct((db, 1, W_ATT), F32),
        compiler_params=_params(("arbitrary", "arbitrary")),
        name="fox_decode",
    )(page_table, *([pool_k] * pg), *([pool_v] * pg), *([pool_lft] * pg), q, k_new, v_new, cn, u)


def _rope_table(pos):
    half = ROPE_DIMS // 2
    inv = ROPE_THETA ** (-jnp.arange(0, ROPE_DIMS, 2, dtype=F32) / ROPE_DIMS)
    ang = pos.astype(F32)[:, None] * inv[None, :]
    d = np.arange(LANES) % HEAD_DIM
    sel = jnp.asarray(d % half)
    cos = jnp.where(jnp.asarray(d < ROPE_DIMS)[None, :], jnp.cos(ang)[:, sel], 1.0)
    sin = jnp.sin(ang)[:, sel]
    sin_up = jnp.where(jnp.asarray(d < half)[None, :], -sin, 0.0)
    sin_dn = jnp.where(jnp.asarray((d >= half) & (d < ROPE_DIMS))[None, :], sin, 0.0)
    return jnp.concatenate([cos, sin_up, sin_dn], axis=1)


def kernel(x_prompt, x_sample, cache_moba_k, cache_moba_v, cache_fox_k, cache_fox_v, cache_fox_logf,
           state_ffn_conv, page_table, p_prompt, p_sample, norm_attn_g, w_in, b_forget,
           qnorm_moba, knorm_moba, qnorm_fox, knorm_fox, w_branch_moba, w_branch_fox, w_out,
           norm_ffn_g, w_up, w_gate, conv_w, conv_b, w_down, norm_ple_g, w_ple, w_ple_gate):
    b, t_len, d_model = x_prompt.shape
    db, ts, _ = x_sample.shape
    depth = w_in.shape[0]
    n_pages = page_table.shape[1]
    n_pool = cache_moba_k.shape[1]
    assert depth == 1 and ts == 1 and t_len % ATT_TILE == 0 and n_pages % 2 == 0
    past_len = n_pages * PAGE_SIZE
    d_ff = w_up.shape[2]
    n_att = 6 * W_ATT

    w_in0 = w_in[0]
    w_fl = jnp.pad(w_in0[:, n_att:n_att + N_HEADS], ((0, 0), (0, LANES - N_HEADS)))
    w1 = _bf(jnp.concatenate([w_in0[:, :n_att], w_in0[:, n_att + N_HEADS:], w_fl], axis=1))
    bfl = jnp.pad(b_forget[0], (0, LANES - N_HEADS))[None, :]
    gains = jnp.stack([jnp.tile(g[0], N_HEADS) for g in (qnorm_moba, knorm_moba, qnorm_fox, knorm_fox)])
    gains = jnp.pad(gains, ((0, SUBLANES - 4), (0, 0)))
    seg_np = (np.arange(W_ATT)[:, None] // HEAD_DIM == np.arange(LANES)[None, :]).astype(np.float32)
    seg = jnp.asarray(seg_np, BF16)
    segt = jnp.asarray(seg_np.T, BF16)
    g_attn = norm_attn_g[0][None, :]
    wbm, wbf, wo = _bf(w_branch_moba[0]), _bf(w_branch_fox[0]), _bf(w_out[0])
    ffn_w = (norm_ffn_g[0][None, :], _bf(w_up[0]), _bf(w_gate[0]),
             jnp.pad(conv_w[0], ((0, SUBLANES - CONV_W), (0, 0))), conv_b[0][None, :], _bf(w_down[0]),
             norm_ple_g[0][None, :], _bf(w_ple_gate[0]), _bf(w_ple[0]))

    tm = 256
    xp = x_prompt.reshape(b * t_len, d_model)
    rope_p = _rope_table(jnp.arange(t_len, dtype=jnp.int32))
    qa, ka, va, qb, kb, vb, lf, c, sga, sgb = _inproj(
        xp, g_attn, w1, bfl, gains, rope_p, seg, segt, tm=tm, tiles_per_seq=t_len // tm, kv_transposed=True)
    r3 = lambda a: a.reshape(b, t_len, a.shape[-1])
    ya = _prompt_attention(r3(qa), ka, va)
    c3 = r3(c)
    c_t = c3[:, :, :N_HEADS].transpose(0, 2, 1)
    yb = _prompt_attention(r3(qb), kb, vb, c3, c_t)
    x1 = _merge(xp, ya.reshape(b * t_len, W_ATT), yb.reshape(b * t_len, W_ATT), sga, sgb, wbm, wbf, wo, tm=tm)
    y_p, tail_p = _ffn(x1, p_prompt[0].reshape(b * t_len, -1), None, *ffn_w, n_seq=b, tm=tm, ck=512)

    heads_t = lambda a: a.reshape(1, b, N_HEADS, HEAD_DIM, t_len).transpose(0, 1, 4, 2, 3)
    heads = lambda a, n: a.reshape(1, n, -1, N_HEADS, HEAD_DIM)
    outs_p = (heads_t(ka), heads_t(va), heads_t(kb), heads_t(vb),
              lf[:, :N_HEADS].reshape(1, b, t_len, N_HEADS), tail_p[None, :, SUBLANES - (CONV_W - 1):, :])

    xs = x_sample.reshape(db, d_model)
    rope_s = _rope_table(jnp.full((db,), past_len, jnp.int32))
    qa, ka, va, qb, kb, vb, lf, _, sga, sgb = _inproj(
        xs, g_attn, w1, bfl, gains, rope_s, seg, segt, tm=db, tiles_per_seq=1, kv_transposed=False)
    v3 = lambda a: a.reshape(db, 1, W_ATT)
    col = lambda a: a.reshape(db, W_ATT, 1)
    pool = lambda cache: cache[0].transpose(0, 2, 3, 1).reshape(n_pool, W_ATT, PAGE_SIZE)
    pool_mk = pool(cache_moba_k)
    top = _moba_scores(page_table, pool_mk, col(qa), pg=min(32, n_pages))
    top_flat = top[:, :, :MOBA_TOPK].reshape(-1)
    ya = _moba_decode(page_table, top_flat, pool_mk, pool(cache_moba_v), col(qa), col(ka), col(va), hps=4)
    cn = jnp.broadcast_to(lf[:, :N_HEADS, None], (db, N_HEADS, LANES))
    yb = _fox_decode(page_table, pool(cache_fox_k), pool(cache_fox_v),
                     cache_fox_logf[0].transpose(0, 2, 1), v3(qb), v3(kb), v3(vb), cn, pg=min(16, n_pages))
    x1 = _merge(xs, ya.reshape(db, W_ATT), yb.reshape(db, W_ATT), sga, sgb, wbm, wbf, wo, tm=db)
    bufs = (state_ffn_conv[0, :, 0, :], state_ffn_conv[0, :, 1, :])
    y_s, a_s = _ffn(x1, p_sample[0].reshape(db, -1), bufs, *ffn_w, n_seq=db, tm=db, ck=512)
    conv_s = jnp.stack([state_ffn_conv[0, :, 1, :], a_s], axis=1)[None]
    outs_s = (heads(ka, db), heads(va, db), heads(kb, db), heads(vb, db),
              lf[:, :N_HEADS].reshape(1, db, ts, N_HEADS), conv_s)

    return (y_p.reshape(b, t_len, d_model), y_s.reshape(db, ts, d_model)) + outs_p + outs_s
```

```python
import functools
from typing import Callable, NamedTuple

import jax
import jax.numpy as jnp
import numpy as np
from jax import lax
from jax.experimental import pallas as pl
from jax.experimental.pallas import tpu as pltpu

F32 = jnp.float32
BF16 = jnp.bfloat16

HEAD_DIM = 64
N_HEADS = 8
W_ATT = N_HEADS * HEAD_DIM
MOBA_BLOCK = 256
MOBA_TOPK = 3
PAGE_SIZE = 128
ROPE_THETA = 500000.0
ROPE_DIMS = HEAD_DIM // 4
CONV_W = 3
EPS = 1e-6
SCALE = HEAD_DIM ** -0.5
NEG = -1e30

LANES = 128
SUBLANES = 8
BLK_PER_TILE = 2
ATT_TILE = BLK_PER_TILE * MOBA_BLOCK
VMEM_LIMIT = 56 * 1024 * 1024


def _bf(x):
    return x.astype(BF16)


def _dot(a, b):
    return jnp.dot(a, b, preferred_element_type=F32)


def _dot_nt(a, b):
    return lax.dot_general(a, b, (((1,), (1,)), ((), ())), preferred_element_type=F32)


def _split2(x):
    hi = _bf(x)
    lo = _bf(x - hi.astype(F32))
    return hi, lo


def _split3(x):
    hi = _bf(x)
    r = x - hi.astype(F32)
    mid = _bf(r)
    lo = _bf(r - mid.astype(F32))
    return hi, mid, lo


def _dot_left_exact(a_bf, x):
    hi, mid, lo = _split3(x)
    return _dot(a_bf, hi) + _dot(a_bf, mid) + _dot(a_bf, lo)


def _dot_right_exact(x, b_bf):
    hi, mid, lo = _split3(x)
    return _dot(hi, b_bf) + _dot(mid, b_bf) + _dot(lo, b_bf)


def _sigmoid(x):
    return 1.0 / (1.0 + jnp.exp(-x))


def _rms(x):
    return x * lax.rsqrt(jnp.mean(x * x, axis=-1, keepdims=True) + EPS)


def _const_spec(shape):
    nd = len(shape)
    return pl.BlockSpec(shape, lambda *_: (0,) * nd, pipeline_mode=pl.Buffered(1))


def _params(sem):
    return pltpu.CompilerParams(dimension_semantics=sem, vmem_limit_bytes=VMEM_LIMIT)


def _inproj_body(x_ref, g_ref, w_ref, bfl_ref, gains_ref, rope_ref, seg_ref, segt_ref, tri_ref,
                 qa_ref, ka_ref, va_ref, qb_ref, kb_ref, vb_ref, lf_ref, c_ref, sga_ref, sgb_ref,
                 carry_ref, *, tiles_per_seq, kv_transposed):
    t = pl.program_id(0)
    tm = x_ref.shape[0]
    d_model = x_ref.shape[1]
    h = _bf(_rms(x_ref[...]) * g_ref[...])
    seg = seg_ref[...]
    segt = segt_ref[...]
    rope = rope_ref[...]

    def widen(tbl):
        return jnp.concatenate([tbl] * (W_ATT // LANES), axis=1)

    cos_t = widen(rope[:, 0:LANES])
    sin_up = widen(rope[:, LANES:2 * LANES])
    sin_dn = widen(rope[:, 2 * LANES:3 * LANES])

    def proj(c0, n):
        return _dot(h, w_ref[:, c0:c0 + n])

    def headnorm(y, gain):
        ss = _dot(_bf(y * y), seg)
        r = lax.rsqrt(ss * (1.0 / HEAD_DIM) + EPS)
        r_hi, r_lo = _split2(r)
        return y * (_dot(r_hi, segt) + _dot(r_lo, segt)) * gain

    def rope_fn(y):
        half = ROPE_DIMS // 2
        return (y * cos_t + pltpu.roll(y, W_ATT - half, 1) * sin_up
                + pltpu.roll(y, half, 1) * sin_dn)

    def put_kv(ref, y):
        if kv_transposed:
            ref[0] = y.T
        else:
            ref[...] = y

    qa_ref[...] = rope_fn(headnorm(proj(0, W_ATT), gains_ref[0:1, :]))
    put_kv(ka_ref, rope_fn(headnorm(proj(W_ATT, W_ATT), gains_ref[1:2, :])))
    put_kv(va_ref, proj(2 * W_ATT, W_ATT))
    qb_ref[...] = headnorm(proj(3 * W_ATT, W_ATT), gains_ref[2:3, :])
    put_kv(kb_ref, headnorm(proj(4 * W_ATT, W_ATT), gains_ref[3:4, :]))
    put_kv(vb_ref, proj(5 * W_ATT, W_ATT))
    sga_ref[...] = _sigmoid(proj(6 * W_ATT, d_model))
    sgb_ref[...] = _sigmoid(proj(6 * W_ATT + d_model, d_model))

    fl = proj(6 * W_ATT + 2 * d_model, LANES) + bfl_ref[...]
    lf = jnp.minimum(fl, 0.0) - jnp.log1p(jnp.exp(-jnp.abs(fl)))
    lf_ref[...] = lf

    @pl.when(t % tiles_per_seq == 0)
    def _():
        carry_ref[...] = jnp.zeros_like(carry_ref)

    c = _dot_left_exact(tri_ref[...], lf) + carry_ref[0:1, :]
    c_ref[...] = c
    carry_ref[...] = jnp.broadcast_to(c[tm - 1:tm, :], carry_ref.shape)


def _inproj(x, g, w1, bfl, gains, rope, seg, segt, *, tm, tiles_per_seq, kv_transposed):
    m, d_model = x.shape
    n_w = w1.shape[1]
    tri = jnp.asarray(np.tril(np.ones((tm, tm), np.float32)), BF16)
    rope_tiles = rope.shape[0] // tm
    row = lambda n: pl.BlockSpec((tm, n), lambda t: (t, 0))
    tok = jax.ShapeDtypeStruct((m, W_ATT), F32)
    if kv_transposed:
        tps = tiles_per_seq
        kv = jax.ShapeDtypeStruct((m // (tps * tm), W_ATT, tps * tm), F32)
        kv_spec = pl.BlockSpec((1, W_ATT, tm), lambda t: (t // tps, 0, t % tps))
    else:
        kv, kv_spec = tok, row(W_ATT)
    outs = [tok, kv, kv, tok, kv, kv] + [jax.ShapeDtypeStruct((m, LANES), F32)] * 2 \
        + [jax.ShapeDtypeStruct((m, d_model), F32)] * 2
    qkv_specs = [row(W_ATT), kv_spec, kv_spec] * 2
    return pl.pallas_call(
        functools.partial(_inproj_body, tiles_per_seq=tiles_per_seq, kv_transposed=kv_transposed),
        grid=(m // tm,),
        in_specs=[row(d_model), _const_spec((1, d_model)), _const_spec((d_model, n_w)),
                  _const_spec((1, LANES)), _const_spec((SUBLANES, W_ATT)),
                  pl.BlockSpec((tm, 3 * LANES), lambda t: (t % rope_tiles, 0)),
                  _const_spec((W_ATT, LANES)), _const_spec((LANES, W_ATT)), _const_spec((tm, tm))],
        out_specs=qkv_specs + [row(LANES)] * 2 + [row(d_model)] * 2,
        out_shape=outs,
        scratch_shapes=[pltpu.VMEM((SUBLANES, LANES), F32)],
        compiler_params=_params(("arbitrary",)),
        name="inproj",
    )(x, g, w1, bfl, gains, rope, seg, segt, tri)


def _softmax_tile(s, carry, v_t):
    m, l, acc = carry
    m_new = jnp.maximum(m, jnp.max(s, axis=0, keepdims=True))
    alpha = jnp.exp(m - m_new)
    p = jnp.exp(s - m_new)
    l = alpha * l + jnp.sum(p, axis=0, keepdims=True)
    acc = alpha * acc + _dot(v_t, _bf(p))
    return m_new, l, acc


def _softmax_init():
    return (jnp.full((1, ATT_TILE), NEG, F32), jnp.zeros((1, ATT_TILE), F32),
            jnp.zeros((HEAD_DIM, ATT_TILE), F32))


def _split3_f32(x):
    hi = _bf(x).astype(F32)
    r = x - hi
    mid = _bf(r).astype(F32)
    return hi, mid, _bf(r - mid).astype(F32)


def _head_rows(hh):
    row = lax.broadcasted_iota(jnp.int32, (LANES, 1), 0)
    return (row >= hh * HEAD_DIM) & (row < (hh + 1) * HEAD_DIM)


def _head_lanes(hh):
    lane = lax.broadcasted_iota(jnp.int32, (1, LANES), 1)
    return (lane >= hh * HEAD_DIM) & (lane < (hh + 1) * HEAD_DIM)


def _causal_mask():
    kpos = lax.broadcasted_iota(jnp.int32, (ATT_TILE, ATT_TILE), 0)
    qpos = lax.broadcasted_iota(jnp.int32, (ATT_TILE, ATT_TILE), 1)
    return kpos <= qpos


def _fox_bias_lanes(c, hp):
    lane = lax.broadcasted_iota(jnp.int32, (1, LANES), 1)
    aug = jnp.zeros(c.shape, F32)
    for hh in range(2):
        ck = jnp.sum(jnp.where(lane == 2 * hp + hh, c, 0.0), axis=1, keepdims=True)
        base = SUBLANES * hh
        for off, part in enumerate(_split3_f32(ck)):
            aug = jnp.where(lane == base + off, -part, aug)
        aug = jnp.where((lane >= base + 3) & (lane < base + 6), 1.0, aug)
    return aug


def _fox_bias_rows(cq, hh):
    row = lax.broadcasted_iota(jnp.int32, (LANES, 1), 0)
    base = SUBLANES * hh
    aug = jnp.where((row >= base) & (row < base + 3), 1.0, jnp.zeros((LANES, cq.shape[1]), F32))
    for off, part in enumerate(_split3_f32(cq)):
        aug = jnp.where(row == base + 3 + off, part, aug)
    return aug


def _moba_select(q_ref, km, sel_ref, ns, nb):
    blk = lax.broadcasted_iota(jnp.int32, (nb, 1), 0)
    qlane = lax.broadcasted_iota(jnp.int32, (1, ATT_TILE), 1)
    for j in range(ns):
        own = jnp.zeros((1, ATT_TILE), jnp.int32) + BLK_PER_TILE * j
        for e in range(1, BLK_PER_TILE):
            own = jnp.where(qlane >= e * MOBA_BLOCK, BLK_PER_TILE * j + e, own)
        past = blk < own
        qt_hi, qt_lo = _split2(q_ref[0, j * ATT_TILE:(j + 1) * ATT_TILE, :].T)
        for hh in range(2):
            km_hi, km_lo = _split2(jnp.where(_head_lanes(hh), km, 0.0))
            s = _dot(km_hi, qt_hi) + _dot(km_hi, qt_lo) + _dot(km_lo, qt_hi)
            sm = jnp.where(past, s, -jnp.inf)
            sel = jnp.zeros((nb, ATT_TILE), F32)
            for n in range(BLK_PER_TILE * (j + 1) - 1):
                sn = sm[n:n + 1, :]
                beats = (sm > sn) | ((sm == sn) & (blk < n))
                rank = jnp.sum(beats.astype(F32), axis=0, keepdims=True)
                ok = (rank < MOBA_TOPK) & (jnp.abs(sn) < jnp.inf)
                sel = jnp.where(blk == n, ok.astype(F32), sel)
            sel_ref[hh, j] = jnp.where(blk == own, 1.0, sel)


def _prompt_attn_body(pt_ref, *refs, ns, nb, fox, dec):
    del pt_ref
    n_in = 5 if fox else 3
    n_sc = 3 if fox else 4
    n_dec_in, n_dec_sc = len(dec.in_specs), len(dec.scratch)
    attn_in, refs = refs[:n_in], refs[n_in:]
    dec_in, refs = refs[:n_dec_in], refs[n_dec_in:]
    (o_ref, dec_o_ref), refs = refs[:2], refs[2:]
    attn_sc, dec_sc = refs[:n_sc], refs[n_sc:]
    assert len(dec_sc) == n_dec_sc
    if fox:
        q_ref, k_ref, v_ref, ct_ref, c_ref = attn_in
        qtm_ref, vt_ref, kb_ref = attn_sc
    else:
        q_ref, k_ref, v_ref = attn_in
        qtm_ref, vt_ref, kb_ref, sel_ref = attn_sc
    hp = pl.program_id(1)
    i = pl.program_id(2)
    dec_j = ((pl.program_id(0) * pl.num_programs(1) + hp) * ns + i) % dec.steps_per_seq

    @pl.when(i == 0)
    def _():
        blk = lax.broadcasted_iota(jnp.int32, (nb, 1), 0)
        km = jnp.zeros((nb, LANES), F32)
        for j in range(ns):
            rows = slice(j * ATT_TILE, (j + 1) * ATT_TILE)
            kf = k_ref[0, :, rows].T
            kb_ref[rows, 0:LANES] = _bf(kf)
            vt_ref[j] = _bf(v_ref[0, :, rows])
            if not fox:
                for e in range(BLK_PER_TILE):
                    ksum = jnp.sum(kf[e * MOBA_BLOCK:(e + 1) * MOBA_BLOCK, :], axis=0, keepdims=True)
                    km = jnp.where(blk == BLK_PER_TILE * j + e, ksum * (1.0 / MOBA_BLOCK), km)
        if fox:
            kb_ref[:, LANES:2 * LANES] = _bf(_fox_bias_lanes(c_ref[0], hp))
        else:
            _moba_select(q_ref, km, sel_ref, ns, nb)
        for j in range(ns):
            rows = slice(j * ATT_TILE, (j + 1) * ATT_TILE)
            qt = (q_ref[0, rows, :] * SCALE).T
            for hh in range(2):
                qtm_ref[hh, j, 0:LANES, :] = _bf(jnp.where(_head_rows(hh), qt, 0.0))
                if fox:
                    cq = ct_ref[0, pl.ds(2 * hp + hh, 1), rows]
                    qtm_ref[hh, j, LANES:2 * LANES, :] = _bf(_fox_bias_rows(cq, hh))

    causal = _causal_mask()

    def tile(n, carry, diag):
        rows = pl.ds(pl.multiple_of(n * ATT_TILE, ATT_TILE), ATT_TILE)
        kn = kb_ref[rows, :]
        new = []
        for hh in range(2):
            s = _dot(kn, qtm_ref[hh, i])
            if fox:
                keep = causal if diag else None
            else:
                keep = jnp.concatenate(
                    [jnp.broadcast_to(sel_ref[hh, i, pl.ds(BLK_PER_TILE * n + e, 1), :],
                                      (MOBA_BLOCK, ATT_TILE)) for e in range(BLK_PER_TILE)], axis=0) > 0.5
                if diag:
                    keep = keep & causal
            if keep is not None:
                s = jnp.where(keep, s, NEG)
            new.append(_softmax_tile(s, carry[hh], vt_ref[n, hh * HEAD_DIM:(hh + 1) * HEAD_DIM, :]))
        return tuple(new)

    dec.main(dec_j, dec_in, dec_sc)
    carry = tile(i, (_softmax_init(), _softmax_init()), True)
    carry = lax.fori_loop(0, i, lambda n, c: tile(n, c, False), carry)
    o_ref[0] = jnp.concatenate([acc / l for _, l, acc in carry], axis=0).T

    @pl.when(dec_j == dec.steps_per_seq - 1)
    def _():
        dec.final(dec_in, dec_o_ref, dec_sc)


class _DecodeStream(NamedTuple):
    steps_per_seq: int
    in_specs: list
    args: tuple
    out_spec: pl.BlockSpec
    out_shape: jax.ShapeDtypeStruct
    scratch: list
    main: Callable
    final: Callable


def _prompt_attention(page_table, dec, q, k, v, c=None, c_t=None):
    b, t_len, _ = q.shape
    ns = t_len // ATT_TILE
    nb = t_len // MOBA_BLOCK
    n_hp = W_ATT // LANES
    fox = c is not None
    kdim = 2 * LANES if fox else LANES
    spd = dec.steps_per_seq
    assert b * n_hp * ns == page_table.shape[0] * spd

    def dec_map(imap):
        def wrapped(bi, hp, i, pt):
            step = (bi * n_hp + hp) * ns + i
            return imap(step // spd, step % spd, pt)
        return wrapped

    dec_in_specs = [pl.BlockSpec(s.block_shape, dec_map(s.index_map)) for s in dec.in_specs]
    dec_out_spec = pl.BlockSpec(dec.out_spec.block_shape, dec_map(dec.out_spec.index_map))
    q_spec = pl.BlockSpec((1, t_len, LANES), lambda bi, hp, i, pt: (bi, 0, hp))
    kv_spec = pl.BlockSpec((1, LANES, t_len), lambda bi, hp, i, pt: (bi, hp, 0))
    in_specs = [q_spec, kv_spec, kv_spec]
    scratch = [pltpu.VMEM((2, ns, kdim, ATT_TILE), BF16), pltpu.VMEM((ns, LANES, ATT_TILE), BF16),
               pltpu.VMEM((t_len, kdim), BF16)]
    if fox:
        in_specs += [pl.BlockSpec((1, N_HEADS, t_len), lambda bi, hp, i, pt: (bi, 0, 0)),
                     pl.BlockSpec((1, t_len, LANES), lambda bi, hp, i, pt: (bi, 0, 0))]
        args = (q, k, v, c_t, c)
    else:
        args = (q, k, v)
        scratch.append(pltpu.VMEM((2, ns, nb, ATT_TILE), F32))
    grid_spec = pltpu.PrefetchScalarGridSpec(
        num_scalar_prefetch=1,
        grid=(b, n_hp, ns),
        in_specs=in_specs + dec_in_specs,
        out_specs=[pl.BlockSpec((1, ATT_TILE, LANES), lambda bi, hp, i, pt: (bi, i, hp)), dec_out_spec],
        scratch_shapes=scratch + list(dec.scratch),
    )
    return pl.pallas_call(
        functools.partial(_prompt_attn_body, ns=ns, nb=nb, fox=fox, dec=dec),
        grid_spec=grid_spec,
        out_shape=[jax.ShapeDtypeStruct((b, t_len, W_ATT), F32), dec.out_shape],
        compiler_params=_params(("arbitrary", "arbitrary", "arbitrary")),
        name="fox_prompt_decode" if fox else "moba_prompt_scores",
    )(page_table, *args, *dec.args)


def _merge_body(x_ref, ya_ref, yb_ref, sga_ref, sgb_ref, wbm_ref, wbf_ref, wo_ref, o_ref):
    merged = (sga_ref[...] * _dot(_bf(ya_ref[...]), wbm_ref[...])
              + sgb_ref[...] * _dot(_bf(yb_ref[...]), wbf_ref[...]))
    o_ref[...] = x_ref[...] + _dot(_bf(merged), wo_ref[...])


def _merge(x, ya, yb, sga, sgb, wbm, wbf, wo, *, tm):
    m, d_model = x.shape
    row = lambda n: pl.BlockSpec((tm, n), lambda t: (t, 0))
    return pl.pallas_call(
        _merge_body,
        grid=(m // tm,),
        in_specs=[row(d_model), row(W_ATT), row(W_ATT), row(d_model), row(d_model),
                  _const_spec(wbm.shape), _const_spec(wbf.shape), _const_spec(wo.shape)],
        out_specs=row(d_model),
        out_shape=jax.ShapeDtypeStruct((m, d_model), F32),
        compiler_params=_params(("arbitrary",)),
        name="merge",
    )(x, ya, yb, sga, sgb, wbm, wbf, wo)


def _ffn_body(*refs, decode, ck):
    if decode:
        (x_ref, p_ref, buf0_ref, buf1_ref, g_ref, wup_ref, wgate_ref, cw_ref, cb_ref, wdown_ref,
         gp_ref, wpg_ref, wple_ref, o_ref, tail_ref) = refs
    else:
        (x_ref, p_ref, g_ref, wup_ref, wgate_ref, cw_ref, cb_ref, wdown_ref,
         gp_ref, wpg_ref, wple_ref, o_ref, tail_ref, carry_ref) = refs

        @pl.when(pl.program_id(1) == 0)
        def _():
            carry_ref[...] = jnp.zeros_like(carry_ref)

    x = x_ref[...]
    tm = x.shape[0]
    d_ff = wup_ref.shape[1]
    h = _bf(_rms(x) * g_ref[...])
    acc = jnp.zeros_like(x)
    for c in range(d_ff // ck):
        cols = slice(c * ck, (c + 1) * ck)
        u = _dot(h, wup_ref[:, cols])
        a = _dot(h, wgate_ref[:, cols])
        if decode:
            prev2 = buf0_ref[:, cols]
            prev1 = buf1_ref[:, cols]
            tail_ref[:, cols] = a
        else:
            ext = jnp.concatenate([carry_ref[:, cols], a], axis=0)
            prev1 = pltpu.roll(ext, 1, 0)[SUBLANES:, :]
            prev2 = pltpu.roll(ext, 2, 0)[SUBLANES:, :]
            carry_ref[:, cols] = a[tm - SUBLANES:, :]
            tail_ref[0, :, cols] = a[tm - SUBLANES:, :]
        conv = cb_ref[:, cols] + cw_ref[0:1, cols] * prev2 + cw_ref[1:2, cols] * prev1 + cw_ref[2:3, cols] * a
        act = conv * _sigmoid(conv) * u
        acc = acc + _dot(_bf(act), wdown_ref[cols, :])
    x = x + acc
    gate = _sigmoid(_dot(_bf(_rms(x) * gp_ref[...]), wpg_ref[...]))
    o_ref[...] = x + _dot(_bf(p_ref[...]), wple_ref[...]) * gate


def _ffn(x, p, bufs, g, wup, wgate, cw, cb, wdown, gp, wpg, wple, *, n_seq, tm, ck):
    m, d_model = x.shape
    d_ff = wup.shape[1]
    ple = p.shape[1]
    decode = bufs is not None
    weights = (g, wup, wgate, cw, cb, wdown, gp, wpg, wple)
    w_specs = [_const_spec(w.shape) for w in weights]
    if decode:
        row = lambda n: pl.BlockSpec((tm, n), lambda t: (t, 0))
        grid = (m // tm,)
        in_specs = [row(d_model), row(ple), row(d_ff), row(d_ff)] + w_specs
        args = (x, p) + tuple(bufs) + weights
        out_specs = [row(d_model), row(d_ff)]
        out_shape = [jax.ShapeDtypeStruct((m, d_model), F32), jax.ShapeDtypeStruct((m, d_ff), F32)]
        scratch = []
        sem = ("arbitrary",)
    else:
        tps = m // n_seq // tm
        row = lambda n: pl.BlockSpec((tm, n), lambda s, t: (s * tps + t, 0))
        grid = (n_seq, tps)
        in_specs = [row(d_model), row(ple)] + w_specs
        args = (x, p) + weights
        out_specs = [row(d_model), pl.BlockSpec((1, SUBLANES, d_ff), lambda s, t: (s, 0, 0))]
        out_shape = [jax.ShapeDtypeStruct((m, d_model), F32),
                     jax.ShapeDtypeStruct((n_seq, SUBLANES, d_ff), F32)]
        scratch = [pltpu.VMEM((SUBLANES, d_ff), F32)]
        sem = ("arbitrary", "arbitrary")
    return pl.pallas_call(
        functools.partial(_ffn_body, decode=decode, ck=ck),
        grid=grid, in_specs=in_specs, out_specs=out_specs, out_shape=out_shape,
        scratch_shapes=scratch, compiler_params=_params(sem),
        name="ffn_decode" if decode else "ffn_prompt",
    )(*args)


def _moba_score_main(j, in_refs, sc_refs, *, pg):
    k_refs, q_ref = in_refs[:pg], in_refs[pg]
    sc_ref, = sc_refs
    bps = pg // 2
    lane = lax.broadcasted_iota(jnp.int32, (1, LANES), 1)

    @pl.when(j == 0)
    def _():
        sc_ref[...] = jnp.full_like(sc_ref, -jnp.inf)

    qb = jnp.broadcast_to(q_ref[0], (W_ATT, PAGE_SIZE))
    sc = sc_ref[...]
    for m in range(bps):
        t = (k_refs[2 * m][0] + k_refs[2 * m + 1][0]) * qb
        per_pos = jnp.sum(t.reshape(N_HEADS, HEAD_DIM, PAGE_SIZE), axis=1)
        score = jnp.sum(per_pos, axis=1, keepdims=True) * (1.0 / MOBA_BLOCK)
        sc = jnp.where(lane == j * bps + m, score, sc)
    sc_ref[...] = sc


def _moba_score_final(in_refs, top_ref, sc_refs):
    del in_refs
    lane = lax.broadcasted_iota(jnp.int32, (1, LANES), 1)
    s = sc_refs[0][...]
    idx = lane.astype(F32)
    top = jnp.zeros((N_HEADS, LANES), F32)
    for r in range(MOBA_TOPK):
        best = jnp.max(s, axis=1, keepdims=True)
        arg = jnp.min(jnp.where(s == best, idx, float(LANES)), axis=1, keepdims=True)
        top = jnp.where(lane == r, arg, top)
        s = jnp.where(idx == arg, -jnp.inf, s)
    top_ref[0] = top.astype(jnp.int32)


def _moba_score_stream(page_table, pool_k, q_col, *, pg):
    db, n_pages = page_table.shape
    assert MOBA_TOPK <= n_pages // 2 <= LANES and pg % 2 == 0 and n_pages % pg == 0
    page_spec = lambda i: pl.BlockSpec((1, W_ATT, PAGE_SIZE), lambda b, j, pt: (pt[b, j * pg + i], 0, 0))
    return _DecodeStream(
        steps_per_seq=n_pages // pg,
        in_specs=[page_spec(i) for i in range(pg)] + [pl.BlockSpec((1, W_ATT, 1), lambda b, j, pt: (b, 0, 0))],
        args=(pool_k,) * pg + (q_col,),
        out_spec=pl.BlockSpec((1, N_HEADS, LANES), lambda b, j, pt: (b, 0, 0)),
        out_shape=jax.ShapeDtypeStruct((db, N_HEADS, LANES), jnp.int32),
        scratch=[pltpu.VMEM((N_HEADS, LANES), F32)],
        main=functools.partial(_moba_score_main, pg=pg),
        final=_moba_score_final,
    )


def _moba_decode_body(pt_ref, top_ref, *refs, n_pg, hps):
    del pt_ref, top_ref
    k_refs = refs[:hps * n_pg]
    v_refs = refs[hps * n_pg:2 * hps * n_pg]
    q_ref, kn_ref, vn_ref, o_ref = refs[2 * hps * n_pg:]
    for hl in range(hps):
        dims = slice(hl * HEAD_DIM, (hl + 1) * HEAD_DIM)
        pages = range(hl * n_pg, (hl + 1) * n_pg)
        q = q_ref[0, dims, :] * SCALE
        logits = [jnp.sum(k_refs[i][0] * q, axis=0, keepdims=True) for i in pages]
        l_new = jnp.sum(q * kn_ref[0, dims, :], axis=0, keepdims=True)
        s_max = logits[0]
        for s in logits[1:]:
            s_max = jnp.maximum(s_max, s)
        m = jnp.maximum(l_new, jnp.max(s_max, axis=1, keepdims=True))
        p_new = jnp.exp(l_new - m)
        p_sum = jnp.zeros((1, PAGE_SIZE), F32)
        pv = jnp.zeros((HEAD_DIM, PAGE_SIZE), F32)
        for i, s in zip(pages, logits):
            p = jnp.exp(s - m)
            p_sum = p_sum + p
            pv = pv + v_refs[i][0] * p
        denom = p_new + jnp.sum(p_sum, axis=1, keepdims=True)
        acc = p_new * vn_ref[0, dims, :] + jnp.sum(pv, axis=1, keepdims=True)
        o_ref[0, dims, :] = acc / denom


def _moba_decode(page_table, top_flat, pool_k, pool_v, q_col, k_col, v_col, *, hps):
    db = page_table.shape[0]
    n_pg = 2 * MOBA_TOPK

    def page_spec(hl, r, e):
        def imap(b, g, pt, top):
            h = g * hps + hl
            blk = top[(b * N_HEADS + h) * MOBA_TOPK + r]
            return (pt[b, 2 * blk + e], h, 0)
        return pl.BlockSpec((1, HEAD_DIM, PAGE_SIZE), imap)

    page_specs = [page_spec(hl, r, e) for hl in range(hps) for r in range(MOBA_TOPK) for e in range(2)]
    vec_spec = pl.BlockSpec((1, hps * HEAD_DIM, 1), lambda b, g, pt, top: (b, g, 0))
    grid_spec = pltpu.PrefetchScalarGridSpec(
        num_scalar_prefetch=2,
        grid=(db, N_HEADS // hps),
        in_specs=page_specs + page_specs + [vec_spec] * 3,
        out_specs=vec_spec,
    )
    return pl.pallas_call(
        functools.partial(_moba_decode_body, n_pg=n_pg, hps=hps),
        grid_spec=grid_spec,
        out_shape=jax.ShapeDtypeStruct((db, W_ATT, 1), F32),
        compiler_params=_params(("arbitrary", "arbitrary")),
        name="moba_decode",
    )(page_table, top_flat, *([pool_k] * (hps * n_pg)), *([pool_v] * (hps * n_pg)), q_col, k_col, v_col)


def _fox_q8(q_ref):
    head = lax.broadcasted_iota(jnp.int32, (N_HEADS, 1), 0)
    lane = lax.broadcasted_iota(jnp.int32, (1, W_ATT), 1)
    own = (lane >= head * HEAD_DIM) & (lane < (head + 1) * HEAD_DIM)
    return jnp.where(own, q_ref[0] * SCALE, 0.0), own


def _fox_decode_main(j, in_refs, sc_refs, *, pg):
    k_refs, v_refs, lf_refs = in_refs[:pg], in_refs[pg:2 * pg], in_refs[2 * pg:3 * pg]
    q_ref, _, _, cn_ref, u_ref = in_refs[3 * pg:]
    m_ref, l_ref, acc_ref, sfx_ref = sc_refs

    @pl.when(j == 0)
    def _():
        m_ref[...] = jnp.full_like(m_ref, NEG)
        l_ref[...] = jnp.zeros_like(l_ref)
        acc_ref[...] = jnp.zeros_like(acc_ref)
        sfx_ref[...] = jnp.zeros_like(sfx_ref)

    q8b = _bf(_fox_q8(q_ref)[0])
    cn = cn_ref[0]
    u = u_ref[...]
    m = m_ref[:, 0:1]
    l = l_ref[:, 0:1]
    acc = acc_ref[...]
    sfx = sfx_ref[...]
    lfts = [lf_refs[i][0] for i in range(pg)]
    within = _dot_right_exact(jnp.concatenate(lfts, axis=0), u)
    logits = []
    for i in range(pg):
        r = within[i * N_HEADS:(i + 1) * N_HEADS, :] + sfx
        logits.append(_dot(q8b, _bf(k_refs[i][0])) + cn + r)
        sfx = sfx + jnp.sum(lfts[i], axis=1, keepdims=True)
    s_max = logits[0]
    for s in logits[1:]:
        s_max = jnp.maximum(s_max, s)
    m_new = jnp.maximum(m, jnp.max(s_max, axis=1, keepdims=True))
    alpha = jnp.exp(m - m_new)
    p_sum = jnp.zeros((N_HEADS, PAGE_SIZE), F32)
    pv = jnp.zeros((N_HEADS, W_ATT), F32)
    for i, s in enumerate(logits):
        p = jnp.exp(s - m_new)
        p_sum = p_sum + p
        pv = pv + _dot_nt(_bf(p), _bf(v_refs[i][0]))
    l = alpha * l + jnp.sum(p_sum, axis=1, keepdims=True)
    acc = alpha * acc + pv
    m = m_new
    m_ref[...] = jnp.broadcast_to(m, m_ref.shape)
    l_ref[...] = jnp.broadcast_to(l, l_ref.shape)
    acc_ref[...] = acc
    sfx_ref[...] = sfx


def _fox_decode_final(in_refs, o_ref, sc_refs):
    q_ref, kn_ref, vn_ref, cn_ref, _ = in_refs[-5:]
    m_ref, l_ref, acc_ref, _ = sc_refs
    q8, own = _fox_q8(q_ref)
    m = m_ref[:, 0:1]
    cn1 = cn_ref[0][:, 0:1]
    l_new = jnp.sum(q8 * kn_ref[0], axis=1, keepdims=True) + cn1 - cn1
    m_fin = jnp.maximum(m, l_new)
    a = jnp.exp(m - m_fin)
    p_new = jnp.exp(l_new - m_fin)
    o8 = (a * acc_ref[...] + p_new * vn_ref[0]) / (a * l_ref[:, 0:1] + p_new)
    o_ref[0] = jnp.sum(jnp.where(own, o8, 0.0), axis=0, keepdims=True)


def _fox_decode_stream(page_table, pool_k, pool_v, pool_lft, q, k_new, v_new, cn, *, pg):
    db, n_pages = page_table.shape
    assert n_pages % pg == 0

    def pidx(b, j, pt, i):
        return pt[b, n_pages - 1 - (j * pg + i)]

    kv_spec = lambda i: pl.BlockSpec((1, W_ATT, PAGE_SIZE), lambda b, j, pt: (pidx(b, j, pt, i), 0, 0))
    lf_spec = lambda i: pl.BlockSpec((1, N_HEADS, PAGE_SIZE), lambda b, j, pt: (pidx(b, j, pt, i), 0, 0))
    vec_spec = pl.BlockSpec((1, 1, W_ATT), lambda b, j, pt: (b, 0, 0))
    u = jnp.asarray(np.tril(np.ones((PAGE_SIZE, PAGE_SIZE), np.float32), -1), BF16)
    return _DecodeStream(
        steps_per_seq=n_pages // pg,
        in_specs=[kv_spec(i) for i in range(pg)] * 2 + [lf_spec(i) for i in range(pg)] + [
            vec_spec, vec_spec, vec_spec,
            pl.BlockSpec((1, N_HEADS, LANES), lambda b, j, pt: (b, 0, 0)),
            pl.BlockSpec((PAGE_SIZE, PAGE_SIZE), lambda b, j, pt: (0, 0))],
        args=(pool_k,) * pg + (pool_v,) * pg + (pool_lft,) * pg + (q, k_new, v_new, cn, u),
        out_spec=vec_spec,
        out_shape=jax.ShapeDtypeStruct((db, 1, W_ATT), F32),
        scratch=[pltpu.VMEM((N_HEADS, LANES), F32), pltpu.VMEM((N_HEADS, LANES), F32),
                 pltpu.VMEM((N_HEADS, W_ATT), F32), pltpu.VMEM((N_HEADS, LANES), F32)],
        main=functools.partial(_fox_decode_main, pg=pg),
        final=_fox_decode_final,
    )


def _rope_table(pos):
    half = ROPE_DIMS // 2
    inv = ROPE_THETA ** (-jnp.arange(0, ROPE_DIMS, 2, dtype=F32) / ROPE_DIMS)
    ang = pos.astype(F32)[:, None] * inv[None, :]
    d = np.arange(LANES) % HEAD_DIM
    sel = jnp.asarray(d % half)
    cos = jnp.where(jnp.asarray(d < ROPE_DIMS)[None, :], jnp.cos(ang)[:, sel], 1.0)
    sin = jnp.sin(ang)[:, sel]
    sin_up = jnp.where(jnp.asarray(d < half)[None, :], -sin, 0.0)
    sin_dn = jnp.where(jnp.asarray((d >= half) & (d < ROPE_DIMS))[None, :], sin, 0.0)
    return jnp.concatenate([cos, sin_up, sin_dn], axis=1)


def kernel(x_prompt, x_sample, cache_moba_k, cache_moba_v, cache_fox_k, cache_fox_v, cache_fox_logf,
           state_ffn_conv, page_table, p_prompt, p_sample, norm_attn_g, w_in, b_forget,
           qnorm_moba, knorm_moba, qnorm_fox, knorm_fox, w_branch_moba, w_branch_fox, w_out,
           norm_ffn_g, w_up, w_gate, conv_w, conv_b, w_down, norm_ple_g, w_ple, w_ple_gate):
    b, t_len, d_model = x_prompt.shape
    db, ts, _ = x_sample.shape
    depth = w_in.shape[0]
    n_pages = page_table.shape[1]
    n_pool = cache_moba_k.shape[1]
    assert depth == 1 and ts == 1 and t_len % ATT_TILE == 0 and n_pages % 2 == 0
    past_len = n_pages * PAGE_SIZE
    d_ff = w_up.shape[2]
    n_att = 6 * W_ATT

    w_in0 = w_in[0]
    w_fl = jnp.pad(w_in0[:, n_att:n_att + N_HEADS], ((0, 0), (0, LANES - N_HEADS)))
    w1 = _bf(jnp.concatenate([w_in0[:, :n_att], w_in0[:, n_att + N_HEADS:], w_fl], axis=1))
    bfl = jnp.pad(b_forget[0], (0, LANES - N_HEADS))[None, :]
    gains = jnp.stack([jnp.tile(g[0], N_HEADS) for g in (qnorm_moba, knorm_moba, qnorm_fox, knorm_fox)])
    gains = jnp.pad(gains, ((0, SUBLANES - 4), (0, 0)))
    seg_np = (np.arange(W_ATT)[:, None] // HEAD_DIM == np.arange(LANES)[None, :]).astype(np.float32)
    seg = jnp.asarray(seg_np, BF16)
    segt = jnp.asarray(seg_np.T, BF16)
    g_attn = norm_attn_g[0][None, :]
    wbm, wbf, wo = _bf(w_branch_moba[0]), _bf(w_branch_fox[0]), _bf(w_out[0])
    ffn_w = (norm_ffn_g[0][None, :], _bf(w_up[0]), _bf(w_gate[0]),
             jnp.pad(conv_w[0], ((0, SUBLANES - CONV_W), (0, 0))), conv_b[0][None, :], _bf(w_down[0]),
             norm_ple_g[0][None, :], _bf(w_ple_gate[0]), _bf(w_ple[0]))

    tm = 256
    xp = x_prompt.reshape(b * t_len, d_model)
    rope_p = _rope_table(jnp.arange(t_len, dtype=jnp.int32))
    qa, ka, va, qb, kb, vb, lf, c, sga, sgb = _inproj(
        xp, g_attn, w1, bfl, gains, rope_p, seg, segt, tm=tm, tiles_per_seq=t_len // tm, kv_transposed=True)
    xs = x_sample.reshape(db, d_model)
    rope_s = _rope_table(jnp.full((db,), past_len, jnp.int32))
    qa_s, ka_s, va_s, qb_s, kb_s, vb_s, lf_s, _, sga_s, sgb_s = _inproj(
        xs, g_attn, w1, bfl, gains, rope_s, seg, segt, tm=db, tiles_per_seq=1, kv_transposed=False)

    attn_steps = b * (W_ATT // LANES) * (t_len // ATT_TILE)
    assert attn_steps % db == 0 and n_pages % (attn_steps // db) == 0
    pg = n_pages // (attn_steps // db)
    v3 = lambda a: a.reshape(db, 1, W_ATT)
    col = lambda a: a.reshape(db, W_ATT, 1)
    pool = lambda cache: cache[0].transpose(0, 2, 3, 1).reshape(n_pool, W_ATT, PAGE_SIZE)
    pool_mk = pool(cache_moba_k)
    r3 = lambda a: a.reshape(b, t_len, a.shape[-1])
    ya, top = _prompt_attention(page_table, _moba_score_stream(page_table, pool_mk, col(qa_s), pg=pg),
                                r3(qa), ka, va)
    c3 = r3(c)
    c_t = c3[:, :, :N_HEADS].transpose(0, 2, 1)
    cn = jnp.broadcast_to(lf_s[:, :N_HEADS, None], (db, N_HEADS, LANES))
    fox_stream = _fox_decode_stream(page_table, pool(cache_fox_k), pool(cache_fox_v),
                                    cache_fox_logf[0].transpose(0, 2, 1), v3(qb_s), v3(kb_s), v3(vb_s), cn, pg=pg)
    yb, yb_s = _prompt_attention(page_table, fox_stream, r3(qb), kb, vb, c3, c_t)
    top_flat = top[:, :, :MOBA_TOPK].reshape(-1)
    ya_s = _moba_decode(page_table, top_flat, pool_mk, pool(cache_moba_v), col(qa_s), col(ka_s), col(va_s), hps=4)

    x1 = _merge(xp, ya.reshape(b * t_len, W_ATT), yb.reshape(b * t_len, W_ATT), sga, sgb, wbm, wbf, wo, tm=tm)
    y_p, tail_p = _ffn(x1, p_prompt[0].reshape(b * t_len, -1), None, *ffn_w, n_seq=b, tm=tm, ck=512)
    x1_s = _merge(xs, ya_s.reshape(db, W_ATT), yb_s.reshape(db, W_ATT), sga_s, sgb_s, wbm, wbf, wo, tm=db)
    bufs = (state_ffn_conv[0, :, 0, :], state_ffn_conv[0, :, 1, :])
    y_s, a_s = _ffn(x1_s, p_sample[0].reshape(db, -1), bufs, *ffn_w, n_seq=db, tm=db, ck=512)

    heads_t = lambda a: a.reshape(1, b, N_HEADS, HEAD_DIM, t_len).transpose(0, 1, 4, 2, 3)
    heads = lambda a: a.reshape(1, db, ts, N_HEADS, HEAD_DIM)
    outs_p = (heads_t(ka), heads_t(va), heads_t(kb), heads_t(vb),
              lf[:, :N_HEADS].reshape(1, b, t_len, N_HEADS), tail_p[None, :, SUBLANES - (CONV_W - 1):, :])
    conv_s = jnp.stack([state_ffn_conv[0, :, 1, :], a_s], axis=1)[None]
    outs_s = (heads(ka_s), heads(va_s), heads(kb_s), heads(vb_s),
              lf_s[:, :N_HEADS].reshape(1, db, ts, N_HEADS), conv_s)
    return (y_p.reshape(b, t_len, d_model), y_s.reshape(db, ts, d_model)) + outs_p + outs_s
```

```python
import functools
from typing import Callable, NamedTuple

import jax
import jax.numpy as jnp
import numpy as np
from jax import lax
from jax.experimental import pallas as pl
from jax.experimental.pallas import tpu as pltpu

F32 = jnp.float32
BF16 = jnp.bfloat16

HEAD_DIM = 64
N_HEADS = 8
W_ATT = N_HEADS * HEAD_DIM
MOBA_BLOCK = 256
MOBA_TOPK = 3
PAGE_SIZE = 128
ROPE_THETA = 500000.0
ROPE_DIMS = HEAD_DIM // 4
CONV_W = 3
EPS = 1e-6
SCALE = HEAD_DIM ** -0.5
NEG = -1e30

LANES = 128
SUBLANES = 8
BLK_PER_TILE = 2
ATT_TILE = BLK_PER_TILE * MOBA_BLOCK
VMEM_LIMIT = 56 * 1024 * 1024


def _bf(x):
    return x.astype(BF16)


def _dot(a, b):
    return jnp.dot(a, b, preferred_element_type=F32)


def _dot_nt(a, b):
    return lax.dot_general(a, b, (((1,), (1,)), ((), ())), preferred_element_type=F32)


def _split2(x):
    hi = _bf(x)
    lo = _bf(x - hi.astype(F32))
    return hi, lo


def _split3(x):
    hi = _bf(x)
    r = x - hi.astype(F32)
    mid = _bf(r)
    lo = _bf(r - mid.astype(F32))
    return hi, mid, lo


def _dot_left_exact(a_bf, x):
    hi, mid, lo = _split3(x)
    return _dot(a_bf, hi) + _dot(a_bf, mid) + _dot(a_bf, lo)


def _dot_right_exact(x, b_bf):
    hi, mid, lo = _split3(x)
    return _dot(hi, b_bf) + _dot(mid, b_bf) + _dot(lo, b_bf)


def _sigmoid(x):
    return 1.0 / (1.0 + jnp.exp(-x))


def _rms(x):
    return x * lax.rsqrt(jnp.mean(x * x, axis=-1, keepdims=True) + EPS)


def _const_spec(shape):
    nd = len(shape)
    return pl.BlockSpec(shape, lambda *_: (0,) * nd, pipeline_mode=pl.Buffered(1))


def _params(sem):
    return pltpu.CompilerParams(dimension_semantics=sem, vmem_limit_bytes=VMEM_LIMIT)


def _inproj_body(x_ref, g_ref, w_ref, bfl_ref, gains_ref, rope_ref, seg_ref, segt_ref, tri_ref,
                 qa_ref, ka_ref, va_ref, qb_ref, kb_ref, vb_ref, lf_ref, c_ref, sga_ref, sgb_ref,
                 carry_ref, *, tiles_per_seq, kv_transposed):
    t = pl.program_id(0)
    tm = x_ref.shape[0]
    d_model = x_ref.shape[1]
    h = _bf(_rms(x_ref[...]) * g_ref[...])
    seg = seg_ref[...]
    segt = segt_ref[...]
    rope = rope_ref[...]

    def widen(tbl):
        return jnp.concatenate([tbl] * (W_ATT // LANES), axis=1)

    cos_t = widen(rope[:, 0:LANES])
    sin_up = widen(rope[:, LANES:2 * LANES])
    sin_dn = widen(rope[:, 2 * LANES:3 * LANES])

    def proj(c0, n):
        return _dot(h, w_ref[:, c0:c0 + n])

    def headnorm(y, gain):
        ss = _dot(_bf(y * y), seg)
        r = lax.rsqrt(ss * (1.0 / HEAD_DIM) + EPS)
        r_hi, r_lo = _split2(r)
        return y * (_dot(r_hi, segt) + _dot(r_lo, segt)) * gain

    def rope_fn(y):
        half = ROPE_DIMS // 2
        return (y * cos_t + pltpu.roll(y, W_ATT - half, 1) * sin_up
                + pltpu.roll(y, half, 1) * sin_dn)

    def put_kv(ref, y):
        if kv_transposed:
            ref[0] = y.T
        else:
            ref[...] = y

    qa_ref[...] = rope_fn(headnorm(proj(0, W_ATT), gains_ref[0:1, :]))
    put_kv(ka_ref, rope_fn(headnorm(proj(W_ATT, W_ATT), gains_ref[1:2, :])))
    put_kv(va_ref, proj(2 * W_ATT, W_ATT))
    qb_ref[...] = headnorm(proj(3 * W_ATT, W_ATT), gains_ref[2:3, :])
    put_kv(kb_ref, headnorm(proj(4 * W_ATT, W_ATT), gains_ref[3:4, :]))
    put_kv(vb_ref, proj(5 * W_ATT, W_ATT))
    sga_ref[...] = _sigmoid(proj(6 * W_ATT, d_model))
    sgb_ref[...] = _sigmoid(proj(6 * W_ATT + d_model, d_model))

    fl = proj(6 * W_ATT + 2 * d_model, LANES) + bfl_ref[...]
    lf = jnp.minimum(fl, 0.0) - jnp.log1p(jnp.exp(-jnp.abs(fl)))
    lf_ref[...] = lf

    @pl.when(t % tiles_per_seq == 0)
    def _():
        carry_ref[...] = jnp.zeros_like(carry_ref)

    c = _dot_left_exact(tri_ref[...], lf) + carry_ref[0:1, :]
    c_ref[...] = c
    carry_ref[...] = jnp.broadcast_to(c[tm - 1:tm, :], carry_ref.shape)


def _inproj(x, g, w1, bfl, gains, rope, seg, segt, *, tm, tiles_per_seq, kv_transposed):
    m, d_model = x.shape
    n_w = w1.shape[1]
    tri = jnp.asarray(np.tril(np.ones((tm, tm), np.float32)), BF16)
    rope_tiles = rope.shape[0] // tm
    row = lambda n: pl.BlockSpec((tm, n), lambda t: (t, 0))
    tok = jax.ShapeDtypeStruct((m, W_ATT), F32)
    if kv_transposed:
        tps = tiles_per_seq
        kv = jax.ShapeDtypeStruct((m // (tps * tm), W_ATT, tps * tm), F32)
        kv_spec = pl.BlockSpec((1, W_ATT, tm), lambda t: (t // tps, 0, t % tps))
    else:
        kv, kv_spec = tok, row(W_ATT)
    outs = [tok, kv, kv, tok, kv, kv] + [jax.ShapeDtypeStruct((m, LANES), F32)] * 2 \
        + [jax.ShapeDtypeStruct((m, d_model), F32)] * 2
    qkv_specs = [row(W_ATT), kv_spec, kv_spec] * 2
    return pl.pallas_call(
        functools.partial(_inproj_body, tiles_per_seq=tiles_per_seq, kv_transposed=kv_transposed),
        grid=(m // tm,),
        in_specs=[row(d_model), _const_spec((1, d_model)), _const_spec((d_model, n_w)),
                  _const_spec((1, LANES)), _const_spec((SUBLANES, W_ATT)),
                  pl.BlockSpec((tm, 3 * LANES), lambda t: (t % rope_tiles, 0)),
                  _const_spec((W_ATT, LANES)), _const_spec((LANES, W_ATT)), _const_spec((tm, tm))],
        out_specs=qkv_specs + [row(LANES)] * 2 + [row(d_model)] * 2,
        out_shape=outs,
        scratch_shapes=[pltpu.VMEM((SUBLANES, LANES), F32)],
        compiler_params=_params(("arbitrary",)),
        name="inproj",
    )(x, g, w1, bfl, gains, rope, seg, segt, tri)


def _softmax_tile(s, carry, v_t):
    m, l, acc = carry
    m_new = jnp.maximum(m, jnp.max(s, axis=0, keepdims=True))
    alpha = jnp.exp(m - m_new)
    p = jnp.exp(s - m_new)
    l = alpha * l + jnp.sum(p, axis=0, keepdims=True)
    acc = alpha * acc + _dot(v_t, _bf(p))
    return m_new, l, acc


def _softmax_init():
    return (jnp.full((1, ATT_TILE), NEG, F32), jnp.zeros((1, ATT_TILE), F32),
            jnp.zeros((HEAD_DIM, ATT_TILE), F32))


def _split3_f32(x):
    hi = _bf(x).astype(F32)
    r = x - hi
    mid = _bf(r).astype(F32)
    return hi, mid, _bf(r - mid).astype(F32)


def _head_rows(hh):
    row = lax.broadcasted_iota(jnp.int32, (LANES, 1), 0)
    return (row >= hh * HEAD_DIM) & (row < (hh + 1) * HEAD_DIM)


def _head_lanes(hh):
    lane = lax.broadcasted_iota(jnp.int32, (1, LANES), 1)
    return (lane >= hh * HEAD_DIM) & (lane < (hh + 1) * HEAD_DIM)


def _causal_mask():
    kpos = lax.broadcasted_iota(jnp.int32, (ATT_TILE, ATT_TILE), 0)
    qpos = lax.broadcasted_iota(jnp.int32, (ATT_TILE, ATT_TILE), 1)
    return kpos <= qpos


def _fox_bias_lanes(c, hp):
    lane = lax.broadcasted_iota(jnp.int32, (1, LANES), 1)
    aug = jnp.zeros(c.shape, F32)
    for hh in range(2):
        ck = jnp.sum(jnp.where(lane == 2 * hp + hh, c, 0.0), axis=1, keepdims=True)
        base = SUBLANES * hh
        for off, part in enumerate(_split3_f32(ck)):
            aug = jnp.where(lane == base + off, -part, aug)
        aug = jnp.where((lane >= base + 3) & (lane < base + 6), 1.0, aug)
    return aug


def _fox_bias_rows(cq, hh):
    row = lax.broadcasted_iota(jnp.int32, (LANES, 1), 0)
    base = SUBLANES * hh
    aug = jnp.where((row >= base) & (row < base + 3), 1.0, jnp.zeros((LANES, cq.shape[1]), F32))
    for off, part in enumerate(_split3_f32(cq)):
        aug = jnp.where(row == base + 3 + off, part, aug)
    return aug


def _moba_select(q_ref, km, sel_ref, ns, nb):
    blk = lax.broadcasted_iota(jnp.int32, (nb, 1), 0)
    qlane = lax.broadcasted_iota(jnp.int32, (1, ATT_TILE), 1)
    for j in range(ns):
        own = jnp.zeros((1, ATT_TILE), jnp.int32) + BLK_PER_TILE * j
        for e in range(1, BLK_PER_TILE):
            own = jnp.where(qlane >= e * MOBA_BLOCK, BLK_PER_TILE * j + e, own)
        past = blk < own
        qt_hi, qt_lo = _split2(q_ref[0, j * ATT_TILE:(j + 1) * ATT_TILE, :].T)
        for hh in range(2):
            km_hi, km_lo = _split2(jnp.where(_head_lanes(hh), km, 0.0))
            s = _dot(km_hi, qt_hi) + _dot(km_hi, qt_lo) + _dot(km_lo, qt_hi)
            sm = jnp.where(past, s, -jnp.inf)
            sel = jnp.zeros((nb, ATT_TILE), F32)
            for n in range(BLK_PER_TILE * (j + 1) - 1):
                sn = sm[n:n + 1, :]
                beats = (sm > sn) | ((sm == sn) & (blk < n))
                rank = jnp.sum(beats.astype(F32), axis=0, keepdims=True)
                ok = (rank < MOBA_TOPK) & (jnp.abs(sn) < jnp.inf)
                sel = jnp.where(blk == n, ok.astype(F32), sel)
            sel_ref[hh, j] = jnp.where(blk == own, 1.0, sel)


def _prompt_attn_body(pt_ref, *refs, ns, nb, fox, dec):
    n_in = 5 if fox else 3
    n_sc = 3 if fox else 4
    attn_in, refs = refs[:n_in], refs[n_in:]
    dec_pools, refs = refs[:len(dec.pools)], refs[len(dec.pools):]
    dec_in, refs = refs[:len(dec.in_specs)], refs[len(dec.in_specs):]
    (o_ref, dec_o_ref), refs = refs[:2], refs[2:]
    attn_sc, dec_sc = refs[:n_sc], refs[n_sc:]
    assert len(dec_sc) == len(dec.scratch)
    if fox:
        q_ref, k_ref, v_ref, ct_ref, c_ref = attn_in
        qtm_ref, vt_ref, kb_ref = attn_sc
    else:
        q_ref, k_ref, v_ref = attn_in
        qtm_ref, vt_ref, kb_ref, sel_ref = attn_sc
    hp = pl.program_id(1)
    i = pl.program_id(2)
    step = (pl.program_id(0) * pl.num_programs(1) + hp) * ns + i
    n_steps = pl.num_programs(0) * pl.num_programs(1) * ns
    dec_j = step % dec.steps_per_seq

    @pl.when(i == 0)
    def _():
        blk = lax.broadcasted_iota(jnp.int32, (nb, 1), 0)
        km = jnp.zeros((nb, LANES), F32)
        for j in range(ns):
            rows = slice(j * ATT_TILE, (j + 1) * ATT_TILE)
            kf = k_ref[0, :, rows].T
            kb_ref[rows, 0:LANES] = _bf(kf)
            vt_ref[j] = _bf(v_ref[0, :, rows])
            if not fox:
                for e in range(BLK_PER_TILE):
                    ksum = jnp.sum(kf[e * MOBA_BLOCK:(e + 1) * MOBA_BLOCK, :], axis=0, keepdims=True)
                    km = jnp.where(blk == BLK_PER_TILE * j + e, ksum * (1.0 / MOBA_BLOCK), km)
        if fox:
            kb_ref[:, LANES:2 * LANES] = _bf(_fox_bias_lanes(c_ref[0], hp))
        else:
            _moba_select(q_ref, km, sel_ref, ns, nb)
        for j in range(ns):
            rows = slice(j * ATT_TILE, (j + 1) * ATT_TILE)
            qt = (q_ref[0, rows, :] * SCALE).T
            for hh in range(2):
                qtm_ref[hh, j, 0:LANES, :] = _bf(jnp.where(_head_rows(hh), qt, 0.0))
                if fox:
                    cq = ct_ref[0, pl.ds(2 * hp + hh, 1), rows]
                    qtm_ref[hh, j, LANES:2 * LANES, :] = _bf(_fox_bias_rows(cq, hh))

    causal = _causal_mask()

    def tile(n, carry, diag):
        rows = pl.ds(pl.multiple_of(n * ATT_TILE, ATT_TILE), ATT_TILE)
        kn = kb_ref[rows, :]
        new = []
        for hh in range(2):
            s = _dot(kn, qtm_ref[hh, i])
            if fox:
                keep = causal if diag else None
            else:
                keep = jnp.concatenate(
                    [jnp.broadcast_to(sel_ref[hh, i, pl.ds(BLK_PER_TILE * n + e, 1), :],
                                      (MOBA_BLOCK, ATT_TILE)) for e in range(BLK_PER_TILE)], axis=0) > 0.5
                if diag:
                    keep = keep & causal
            if keep is not None:
                s = jnp.where(keep, s, NEG)
            new.append(_softmax_tile(s, carry[hh], vt_ref[n, hh * HEAD_DIM:(hh + 1) * HEAD_DIM, :]))
        return tuple(new)

    dec.main(pt_ref, step, n_steps, dec_pools, dec_in, dec_sc)
    carry = tile(i, (_softmax_init(), _softmax_init()), True)
    carry = lax.fori_loop(0, i, lambda n, c: tile(n, c, False), carry)
    o_ref[0] = jnp.concatenate([acc / l for _, l, acc in carry], axis=0).T

    @pl.when(dec_j == dec.steps_per_seq - 1)
    def _():
        dec.final(dec_in, dec_o_ref, dec_sc)


class _DecodeStream(NamedTuple):
    steps_per_seq: int
    pools: tuple
    in_specs: list
    args: tuple
    out_spec: pl.BlockSpec
    out_shape: jax.ShapeDtypeStruct
    scratch: list
    main: Callable
    final: Callable


def _page_copies(pt_ref, step, slot, pools, bufs, sems, *, spd, pg, last_first):
    n_pages = pt_ref.shape[1]
    seq, j = step // spd, step % spd
    copies = []
    for i in range(pg):
        col = n_pages - 1 - (j * pg + i) if last_first else j * pg + i
        page = pt_ref[seq, col]
        for pool, buf, sem in zip(pools, bufs, sems):
            copies.append(pltpu.make_async_copy(pool.at[page], buf.at[slot, i], sem.at[slot]))
    return copies


def _stream_pages(pt_ref, step, n_steps, pools, bufs, sems, **kw):
    slot = step % 2

    @pl.when(step == 0)
    def _():
        for c in _page_copies(pt_ref, step, slot, pools, bufs, sems, **kw):
            c.start()

    @pl.when(step + 1 < n_steps)
    def _():
        for c in _page_copies(pt_ref, step + 1, 1 - slot, pools, bufs, sems, **kw):
            c.start()

    for c in _page_copies(pt_ref, step, slot, pools, bufs, sems, **kw):
        c.wait()
    return slot


def _prompt_attention(page_table, dec, q, k, v, c=None, c_t=None):
    b, t_len, _ = q.shape
    ns = t_len // ATT_TILE
    nb = t_len // MOBA_BLOCK
    n_hp = W_ATT // LANES
    fox = c is not None
    kdim = 2 * LANES if fox else LANES
    spd = dec.steps_per_seq
    assert b * n_hp * ns == page_table.shape[0] * spd

    def dec_map(imap):
        def wrapped(bi, hp, i, pt):
            step = (bi * n_hp + hp) * ns + i
            return imap(step // spd, step % spd, pt)
        return wrapped

    dec_in_specs = [pl.BlockSpec(memory_space=pl.ANY)] * len(dec.pools)
    dec_in_specs += [pl.BlockSpec(s.block_shape, dec_map(s.index_map)) for s in dec.in_specs]
    dec_out_spec = pl.BlockSpec(dec.out_spec.block_shape, dec_map(dec.out_spec.index_map))
    q_spec = pl.BlockSpec((1, t_len, LANES), lambda bi, hp, i, pt: (bi, 0, hp))
    kv_spec = pl.BlockSpec((1, LANES, t_len), lambda bi, hp, i, pt: (bi, hp, 0))
    in_specs = [q_spec, kv_spec, kv_spec]
    scratch = [pltpu.VMEM((2, ns, kdim, ATT_TILE), BF16), pltpu.VMEM((ns, LANES, ATT_TILE), BF16),
               pltpu.VMEM((t_len, kdim), BF16)]
    if fox:
        in_specs += [pl.BlockSpec((1, N_HEADS, t_len), lambda bi, hp, i, pt: (bi, 0, 0)),
                     pl.BlockSpec((1, t_len, LANES), lambda bi, hp, i, pt: (bi, 0, 0))]
        args = (q, k, v, c_t, c)
    else:
        args = (q, k, v)
        scratch.append(pltpu.VMEM((2, ns, nb, ATT_TILE), F32))
    grid_spec = pltpu.PrefetchScalarGridSpec(
        num_scalar_prefetch=1,
        grid=(b, n_hp, ns),
        in_specs=in_specs + dec_in_specs,
        out_specs=[pl.BlockSpec((1, ATT_TILE, LANES), lambda bi, hp, i, pt: (bi, i, hp)), dec_out_spec],
        scratch_shapes=scratch + list(dec.scratch),
    )
    return pl.pallas_call(
        functools.partial(_prompt_attn_body, ns=ns, nb=nb, fox=fox, dec=dec),
        grid_spec=grid_spec,
        out_shape=[jax.ShapeDtypeStruct((b, t_len, W_ATT), F32), dec.out_shape],
        compiler_params=_params(("arbitrary", "arbitrary", "arbitrary")),
        name="fox_prompt_decode" if fox else "moba_prompt_scores",
    )(page_table, *args, *dec.pools, *dec.args)


def _merge_body(x_ref, ya_ref, yb_ref, sga_ref, sgb_ref, wbm_ref, wbf_ref, wo_ref, o_ref):
    merged = (sga_ref[...] * _dot(_bf(ya_ref[...]), wbm_ref[...])
              + sgb_ref[...] * _dot(_bf(yb_ref[...]), wbf_ref[...]))
    o_ref[...] = x_ref[...] + _dot(_bf(merged), wo_ref[...])


def _merge(x, ya, yb, sga, sgb, wbm, wbf, wo, *, tm):
    m, d_model = x.shape
    row = lambda n: pl.BlockSpec((tm, n), lambda t: (t, 0))
    return pl.pallas_call(
        _merge_body,
        grid=(m // tm,),
        in_specs=[row(d_model), row(W_ATT), row(W_ATT), row(d_model), row(d_model),
                  _const_spec(wbm.shape), _const_spec(wbf.shape), _const_spec(wo.shape)],
        out_specs=row(d_model),
        out_shape=jax.ShapeDtypeStruct((m, d_model), F32),
        compiler_params=_params(("arbitrary",)),
        name="merge",
    )(x, ya, yb, sga, sgb, wbm, wbf, wo)


def _ffn_body(*refs, decode, ck):
    if decode:
        (x_ref, p_ref, buf0_ref, buf1_ref, g_ref, wup_ref, wgate_ref, cw_ref, cb_ref, wdown_ref,
         gp_ref, wpg_ref, wple_ref, o_ref, tail_ref) = refs
    else:
        (x_ref, p_ref, g_ref, wup_ref, wgate_ref, cw_ref, cb_ref, wdown_ref,
         gp_ref, wpg_ref, wple_ref, o_ref, tail_ref, carry_ref) = refs

        @pl.when(pl.program_id(1) == 0)
        def _():
            carry_ref[...] = jnp.zeros_like(carry_ref)

    x = x_ref[...]
    tm = x.shape[0]
    d_ff = wup_ref.shape[1]
    h = _bf(_rms(x) * g_ref[...])
    acc = jnp.zeros_like(x)
    for c in range(d_ff // ck):
        cols = slice(c * ck, (c + 1) * ck)
        u = _dot(h, wup_ref[:, cols])
        a = _dot(h, wgate_ref[:, cols])
        if decode:
            prev2 = buf0_ref[:, cols]
            prev1 = buf1_ref[:, cols]
            tail_ref[:, cols] = a
        else:
            ext = jnp.concatenate([carry_ref[:, cols], a], axis=0)
            prev1 = pltpu.roll(ext, 1, 0)[SUBLANES:, :]
            prev2 = pltpu.roll(ext, 2, 0)[SUBLANES:, :]
            carry_ref[:, cols] = a[tm - SUBLANES:, :]
            tail_ref[0, :, cols] = a[tm - SUBLANES:, :]
        conv = cb_ref[:, cols] + cw_ref[0:1, cols] * prev2 + cw_ref[1:2, cols] * prev1 + cw_ref[2:3, cols] * a
        act = conv * _sigmoid(conv) * u
        acc = acc + _dot(_bf(act), wdown_ref[cols, :])
    x = x + acc
    gate = _sigmoid(_dot(_bf(_rms(x) * gp_ref[...]), wpg_ref[...]))
    o_ref[...] = x + _dot(_bf(p_ref[...]), wple_ref[...]) * gate


def _ffn(x, p, bufs, g, wup, wgate, cw, cb, wdown, gp, wpg, wple, *, n_seq, tm, ck):
    m, d_model = x.shape
    d_ff = wup.shape[1]
    ple = p.shape[1]
    decode = bufs is not None
    weights = (g, wup, wgate, cw, cb, wdown, gp, wpg, wple)
    w_specs = [_const_spec(w.shape) for w in weights]
    if decode:
        row = lambda n: pl.BlockSpec((tm, n), lambda t: (t, 0))
        grid = (m // tm,)
        in_specs = [row(d_model), row(ple), row(d_ff), row(d_ff)] + w_specs
        args = (x, p) + tuple(bufs) + weights
        out_specs = [row(d_model), row(d_ff)]
        out_shape = [jax.ShapeDtypeStruct((m, d_model), F32), jax.ShapeDtypeStruct((m, d_ff), F32)]
        scratch = []
        sem = ("arbitrary",)
    else:
        tps = m // n_seq // tm
        row = lambda n: pl.BlockSpec((tm, n), lambda s, t: (s * tps + t, 0))
        grid = (n_seq, tps)
        in_specs = [row(d_model), row(ple)] + w_specs
        args = (x, p) + weights
        out_specs = [row(d_model), pl.BlockSpec((1, SUBLANES, d_ff), lambda s, t: (s, 0, 0))]
        out_shape = [jax.ShapeDtypeStruct((m, d_model), F32),
                     jax.ShapeDtypeStruct((n_seq, SUBLANES, d_ff), F32)]
        scratch = [pltpu.VMEM((SUBLANES, d_ff), F32)]
        sem = ("arbitrary", "arbitrary")
    return pl.pallas_call(
        functools.partial(_ffn_body, decode=decode, ck=ck),
        grid=grid, in_specs=in_specs, out_specs=out_specs, out_shape=out_shape,
        scratch_shapes=scratch, compiler_params=_params(sem),
        name="ffn_decode" if decode else "ffn_prompt",
    )(*args)


def _moba_score_main(pt_ref, step, n_steps, pools, in_refs, sc_refs, *, spd, pg):
    q_ref, = in_refs
    sc_ref, kbuf, sem = sc_refs
    slot = _stream_pages(pt_ref, step, n_steps, pools, (kbuf,), (sem,), spd=spd, pg=pg, last_first=False)
    j = step % spd
    bps = pg // 2
    lane = lax.broadcasted_iota(jnp.int32, (1, LANES), 1)

    @pl.when(j == 0)
    def _():
        sc_ref[...] = jnp.full_like(sc_ref, -jnp.inf)

    qb = jnp.broadcast_to(q_ref[0], (W_ATT, PAGE_SIZE))
    sc = sc_ref[...]
    for m in range(bps):
        t = (kbuf[slot, 2 * m] + kbuf[slot, 2 * m + 1]) * qb
        per_pos = jnp.sum(t.reshape(N_HEADS, HEAD_DIM, PAGE_SIZE), axis=1)
        score = jnp.sum(per_pos, axis=1, keepdims=True) * (1.0 / MOBA_BLOCK)
        sc = jnp.where(lane == j * bps + m, score, sc)
    sc_ref[...] = sc


def _moba_score_final(in_refs, top_ref, sc_refs):
    del in_refs
    lane = lax.broadcasted_iota(jnp.int32, (1, LANES), 1)
    s = sc_refs[0][...]
    idx = lane.astype(F32)
    top = jnp.zeros((N_HEADS, LANES), F32)
    for r in range(MOBA_TOPK):
        best = jnp.max(s, axis=1, keepdims=True)
        arg = jnp.min(jnp.where(s == best, idx, float(LANES)), axis=1, keepdims=True)
        top = jnp.where(lane == r, arg, top)
        s = jnp.where(idx == arg, -jnp.inf, s)
    top_ref[0] = top.astype(jnp.int32)


def _moba_score_stream(page_table, pool_k, q_col, *, pg):
    db, n_pages = page_table.shape
    assert MOBA_TOPK <= n_pages // 2 <= LANES and pg % 2 == 0 and n_pages % pg == 0
    spd = n_pages // pg
    return _DecodeStream(
        steps_per_seq=spd,
        pools=(pool_k,),
        in_specs=[pl.BlockSpec((1, W_ATT, 1), lambda b, j, pt: (b, 0, 0))],
        args=(q_col,),
        out_spec=pl.BlockSpec((1, N_HEADS, LANES), lambda b, j, pt: (b, 0, 0)),
        out_shape=jax.ShapeDtypeStruct((db, N_HEADS, LANES), jnp.int32),
        scratch=[pltpu.VMEM((N_HEADS, LANES), F32), pltpu.VMEM((2, pg, W_ATT, PAGE_SIZE), F32),
                 pltpu.SemaphoreType.DMA((2,))],
        main=functools.partial(_moba_score_main, spd=spd, pg=pg),
        final=_moba_score_final,
    )


def _moba_decode_body(pt_ref, top_ref, *refs, n_pg, hps):
    del pt_ref, top_ref
    k_refs = refs[:hps * n_pg]
    v_refs = refs[hps * n_pg:2 * hps * n_pg]
    q_ref, kn_ref, vn_ref, o_ref = refs[2 * hps * n_pg:]
    for hl in range(hps):
        dims = slice(hl * HEAD_DIM, (hl + 1) * HEAD_DIM)
        pages = range(hl * n_pg, (hl + 1) * n_pg)
        q = q_ref[0, dims, :] * SCALE
        logits = [jnp.sum(k_refs[i][0] * q, axis=0, keepdims=True) for i in pages]
        l_new = jnp.sum(q * kn_ref[0, dims, :], axis=0, keepdims=True)
        s_max = logits[0]
        for s in logits[1:]:
            s_max = jnp.maximum(s_max, s)
        m = jnp.maximum(l_new, jnp.max(s_max, axis=1, keepdims=True))
        p_new = jnp.exp(l_new - m)
        p_sum = jnp.zeros((1, PAGE_SIZE), F32)
        pv = jnp.zeros((HEAD_DIM, PAGE_SIZE), F32)
        for i, s in zip(pages, logits):
            p = jnp.exp(s - m)
            p_sum = p_sum + p
            pv = pv + v_refs[i][0] * p
        denom = p_new + jnp.sum(p_sum, axis=1, keepdims=True)
        acc = p_new * vn_ref[0, dims, :] + jnp.sum(pv, axis=1, keepdims=True)
        o_ref[0, dims, :] = acc / denom


def _moba_decode(page_table, top_flat, pool_k, pool_v, q_col, k_col, v_col, *, hps):
    db = page_table.shape[0]
    n_pg = 2 * MOBA_TOPK

    def page_spec(hl, r, e):
        def imap(b, g, pt, top):
            h = g * hps + hl
            blk = top[(b * N_HEADS + h) * MOBA_TOPK + r]
            return (pt[b, 2 * blk + e], h, 0)
        return pl.BlockSpec((1, HEAD_DIM, PAGE_SIZE), imap)

    page_specs = [page_spec(hl, r, e) for hl in range(hps) for r in range(MOBA_TOPK) for e in range(2)]
    vec_spec = pl.BlockSpec((1, hps * HEAD_DIM, 1), lambda b, g, pt, top: (b, g, 0))
    grid_spec = pltpu.PrefetchScalarGridSpec(
        num_scalar_prefetch=2,
        grid=(db, N_HEADS // hps),
        in_specs=page_specs + page_specs + [vec_spec] * 3,
        out_specs=vec_spec,
    )
    return pl.pallas_call(
        functools.partial(_moba_decode_body, n_pg=n_pg, hps=hps),
        grid_spec=grid_spec,
        out_shape=jax.ShapeDtypeStruct((db, W_ATT, 1), F32),
        compiler_params=_params(("arbitrary", "arbitrary")),
        name="moba_decode",
    )(page_table, top_flat, *([pool_k] * (hps * n_pg)), *([pool_v] * (hps * n_pg)), q_col, k_col, v_col)


def _fox_q8(q_ref):
    head = lax.broadcasted_iota(jnp.int32, (N_HEADS, 1), 0)
    lane = lax.broadcasted_iota(jnp.int32, (1, W_ATT), 1)
    own = (lane >= head * HEAD_DIM) & (lane < (head + 1) * HEAD_DIM)
    return jnp.where(own, q_ref[0] * SCALE, 0.0), own


def _fox_decode_main(pt_ref, step, n_steps, pools, in_refs, sc_refs, *, spd, pg):
    q_ref, _, _, cn_ref, u_ref = in_refs
    m_ref, l_ref, acc_ref, sfx_ref, kbuf, vbuf, lfbuf, ksem, vsem, lfsem = sc_refs
    slot = _stream_pages(pt_ref, step, n_steps, pools, (kbuf, vbuf, lfbuf), (ksem, vsem, lfsem),
                         spd=spd, pg=pg, last_first=True)

    @pl.when(step % spd == 0)
    def _():
        m_ref[...] = jnp.full_like(m_ref, NEG)
        l_ref[...] = jnp.zeros_like(l_ref)
        acc_ref[...] = jnp.zeros_like(acc_ref)
        sfx_ref[...] = jnp.zeros_like(sfx_ref)

    q8b = _bf(_fox_q8(q_ref)[0])
    cn = cn_ref[0]
    m = m_ref[:, 0:1]
    l = l_ref[:, 0:1]
    sfx = sfx_ref[...]
    lfts = [lfbuf[slot, i] for i in range(pg)]
    within = _dot_right_exact(jnp.concatenate(lfts, axis=0), u_ref[...])
    logits = []
    for i in range(pg):
        r = within[i * N_HEADS:(i + 1) * N_HEADS, :] + sfx
        logits.append(_dot(q8b, _bf(kbuf[slot, i])) + cn + r)
        sfx = sfx + jnp.sum(lfts[i], axis=1, keepdims=True)
    s_max = logits[0]
    for s in logits[1:]:
        s_max = jnp.maximum(s_max, s)
    m_new = jnp.maximum(m, jnp.max(s_max, axis=1, keepdims=True))
    alpha = jnp.exp(m - m_new)
    p_sum = jnp.zeros((N_HEADS, PAGE_SIZE), F32)
    pv = jnp.zeros((N_HEADS, W_ATT), F32)
    for i, s in enumerate(logits):
        p = jnp.exp(s - m_new)
        p_sum = p_sum + p
        pv = pv + _dot_nt(_bf(p), _bf(vbuf[slot, i]))
    l = alpha * l + jnp.sum(p_sum, axis=1, keepdims=True)
    acc = alpha * acc_ref[...] + pv
    m = m_new
    m_ref[...] = jnp.broadcast_to(m, m_ref.shape)
    l_ref[...] = jnp.broadcast_to(l, l_ref.shape)
    acc_ref[...] = acc
    sfx_ref[...] = sfx


def _fox_decode_final(in_refs, o_ref, sc_refs):
    q_ref, kn_ref, vn_ref, cn_ref, _ = in_refs[-5:]
    m_ref, l_ref, acc_ref = sc_refs[:3]
    q8, own = _fox_q8(q_ref)
    m = m_ref[:, 0:1]
    cn1 = cn_ref[0][:, 0:1]
    l_new = jnp.sum(q8 * kn_ref[0], axis=1, keepdims=True) + cn1 - cn1
    m_fin = jnp.maximum(m, l_new)
    a = jnp.exp(m - m_fin)
    p_new = jnp.exp(l_new - m_fin)
    o8 = (a * acc_ref[...] + p_new * vn_ref[0]) / (a * l_ref[:, 0:1] + p_new)
    o_ref[0] = jnp.sum(jnp.where(own, o8, 0.0), axis=0, keepdims=True)


def _fox_decode_stream(page_table, pool_k, pool_v, pool_lft, q, k_new, v_new, cn, *, pg):
    db, n_pages = page_table.shape
    assert n_pages % pg == 0

    spd = n_pages // pg
    vec_spec = pl.BlockSpec((1, 1, W_ATT), lambda b, j, pt: (b, 0, 0))
    u = jnp.asarray(np.tril(np.ones((PAGE_SIZE, PAGE_SIZE), np.float32), -1), BF16)
    page_buf = pltpu.VMEM((2, pg, W_ATT, PAGE_SIZE), F32)
    return _DecodeStream(
        steps_per_seq=spd,
        pools=(pool_k, pool_v, pool_lft),
        in_specs=[vec_spec, vec_spec, vec_spec,
                  pl.BlockSpec((1, N_HEADS, LANES), lambda b, j, pt: (b, 0, 0)),
                  pl.BlockSpec((PAGE_SIZE, PAGE_SIZE), lambda b, j, pt: (0, 0))],
        args=(q, k_new, v_new, cn, u),
        out_spec=vec_spec,
        out_shape=jax.ShapeDtypeStruct((db, 1, W_ATT), F32),
        scratch=[pltpu.VMEM((N_HEADS, LANES), F32), pltpu.VMEM((N_HEADS, LANES), F32),
                 pltpu.VMEM((N_HEADS, W_ATT), F32), pltpu.VMEM((N_HEADS, LANES), F32),
                 page_buf, page_buf, pltpu.VMEM((2, pg, N_HEADS, PAGE_SIZE), F32)]
        + [pltpu.SemaphoreType.DMA((2,))] * 3,
        main=functools.partial(_fox_decode_main, spd=spd, pg=pg),
        final=_fox_decode_final,
    )


def _rope_table(pos):
    half = ROPE_DIMS // 2
    inv = ROPE_THETA ** (-jnp.arange(0, ROPE_DIMS, 2, dtype=F32) / ROPE_DIMS)
    ang = pos.astype(F32)[:, None] * inv[None, :]
    d = np.arange(LANES) % HEAD_DIM
    sel = jnp.asarray(d % half)
    cos = jnp.where(jnp.asarray(d < ROPE_DIMS)[None, :], jnp.cos(ang)[:, sel], 1.0)
    sin = jnp.sin(ang)[:, sel]
    sin_up = jnp.where(jnp.asarray(d < half)[None, :], -sin, 0.0)
    sin_dn = jnp.where(jnp.asarray((d >= half) & (d < ROPE_DIMS))[None, :], sin, 0.0)
    return jnp.concatenate([cos, sin_up, sin_dn], axis=1)


def kernel(x_prompt, x_sample, cache_moba_k, cache_moba_v, cache_fox_k, cache_fox_v, cache_fox_logf,
           state_ffn_conv, page_table, p_prompt, p_sample, norm_attn_g, w_in, b_forget,
           qnorm_moba, knorm_moba, qnorm_fox, knorm_fox, w_branch_moba, w_branch_fox, w_out,
           norm_ffn_g, w_up, w_gate, conv_w, conv_b, w_down, norm_ple_g, w_ple, w_ple_gate):
    b, t_len, d_model = x_prompt.shape
    db, ts, _ = x_sample.shape
    depth = w_in.shape[0]
    n_pages = page_table.shape[1]
    n_pool = cache_moba_k.shape[1]
    assert depth == 1 and ts == 1 and t_len % ATT_TILE == 0 and n_pages % 2 == 0
    past_len = n_pages * PAGE_SIZE
    d_ff = w_up.shape[2]
    n_att = 6 * W_ATT

    w_in0 = w_in[0]
    w_fl = jnp.pad(w_in0[:, n_att:n_att + N_HEADS], ((0, 0), (0, LANES - N_HEADS)))
    w1 = _bf(jnp.concatenate([w_in0[:, :n_att], w_in0[:, n_att + N_HEADS:], w_fl], axis=1))
    bfl = jnp.pad(b_forget[0], (0, LANES - N_HEADS))[None, :]
    gains = jnp.stack([jnp.tile(g[0], N_HEADS) for g in (qnorm_moba, knorm_moba, qnorm_fox, knorm_fox)])
    gains = jnp.pad(gains, ((0, SUBLANES - 4), (0, 0)))
    seg_np = (np.arange(W_ATT)[:, None] // HEAD_DIM == np.arange(LANES)[None, :]).astype(np.float32)
    seg = jnp.asarray(seg_np, BF16)
    segt = jnp.asarray(seg_np.T, BF16)
    g_attn = norm_attn_g[0][None, :]
    wbm, wbf, wo = _bf(w_branch_moba[0]), _bf(w_branch_fox[0]), _bf(w_out[0])
    ffn_w = (norm_ffn_g[0][None, :], _bf(w_up[0]), _bf(w_gate[0]),
             jnp.pad(conv_w[0], ((0, SUBLANES - CONV_W), (0, 0))), conv_b[0][None, :], _bf(w_down[0]),
             norm_ple_g[0][None, :], _bf(w_ple_gate[0]), _bf(w_ple[0]))

    tm = 256
    xp = x_prompt.reshape(b * t_len, d_model)
    rope_p = _rope_table(jnp.arange(t_len, dtype=jnp.int32))
    qa, ka, va, qb, kb, vb, lf, c, sga, sgb = _inproj(
        xp, g_attn, w1, bfl, gains, rope_p, seg, segt, tm=tm, tiles_per_seq=t_len // tm, kv_transposed=True)
    xs = x_sample.reshape(db, d_model)
    rope_s = _rope_table(jnp.full((db,), past_len, jnp.int32))
    qa_s, ka_s, va_s, qb_s, kb_s, vb_s, lf_s, _, sga_s, sgb_s = _inproj(
        xs, g_attn, w1, bfl, gains, rope_s, seg, segt, tm=db, tiles_per_seq=1, kv_transposed=False)

    attn_steps = b * (W_ATT // LANES) * (t_len // ATT_TILE)
    assert attn_steps % db == 0 and n_pages % (attn_steps // db) == 0
    pg = n_pages // (attn_steps // db)
    v3 = lambda a: a.reshape(db, 1, W_ATT)
    col = lambda a: a.reshape(db, W_ATT, 1)
    pool = lambda cache: cache[0].transpose(0, 2, 3, 1).reshape(n_pool, W_ATT, PAGE_SIZE)
    pool_mk = pool(cache_moba_k)
    r3 = lambda a: a.reshape(b, t_len, a.shape[-1])
    ya, top = _prompt_attention(page_table, _moba_score_stream(page_table, pool_mk, col(qa_s), pg=pg),
                                r3(qa), ka, va)
    c3 = r3(c)
    c_t = c3[:, :, :N_HEADS].transpose(0, 2, 1)
    cn = jnp.broadcast_to(lf_s[:, :N_HEADS, None], (db, N_HEADS, LANES))
    fox_stream = _fox_decode_stream(page_table, pool(cache_fox_k), pool(cache_fox_v),
                                    cache_fox_logf[0].transpose(0, 2, 1), v3(qb_s), v3(kb_s), v3(vb_s), cn, pg=pg)
    yb, yb_s = _prompt_attention(page_table, fox_stream, r3(qb), kb, vb, c3, c_t)
    top_flat = top[:, :, :MOBA_TOPK].reshape(-1)
    ya_s = _moba_decode(page_table, top_flat, pool_mk, pool(cache_moba_v), col(qa_s), col(ka_s), col(va_s), hps=4)

    x1 = _merge(xp, ya.reshape(b * t_len, W_ATT), yb.reshape(b * t_len, W_ATT), sga, sgb, wbm, wbf, wo, tm=tm)
    y_p, tail_p = _ffn(x1, p_prompt[0].reshape(b * t_len, -1), None, *ffn_w, n_seq=b, tm=tm, ck=512)
    x1_s = _merge(xs, ya_s.reshape(db, W_ATT), yb_s.reshape(db, W_ATT), sga_s, sgb_s, wbm, wbf, wo, tm=db)
    bufs = (state_ffn_conv[0, :, 0, :], state_ffn_conv[0, :, 1, :])
    y_s, a_s = _ffn(x1_s, p_sample[0].reshape(db, -1), bufs, *ffn_w, n_seq=db, tm=db, ck=512)

    heads_t = lambda a: a.reshape(1, b, N_HEADS, HEAD_DIM, t_len).transpose(0, 1, 4, 2, 3)
    heads = lambda a: a.reshape(1, db, ts, N_HEADS, HEAD_DIM)
    outs_p = (heads_t(ka), heads_t(va), heads_t(kb), heads_t(vb),
              lf[:, :N_HEADS].reshape(1, b, t_len, N_HEADS), tail_p[None, :, SUBLANES - (CONV_W - 1):, :])
    conv_s = jnp.stack([state_ffn_conv[0, :, 1, :], a_s], axis=1)[None]
    outs_s = (heads(ka_s), heads(va_s), heads(kb_s), heads(vb_s),
              lf_s[:, :N_HEADS].reshape(1, db, ts, N_HEADS), conv_s)
    return (y_p.reshape(b, t_len, d_model), y_s.reshape(db, ts, d_model)) + outs_p + outs_s
```

```python
import functools
from typing import Callable, NamedTuple

import jax
import jax.numpy as jnp
import numpy as np
from jax import lax
from jax.experimental import pallas as pl
from jax.experimental.pallas import tpu as pltpu

F32 = jnp.float32
BF16 = jnp.bfloat16

HEAD_DIM = 64
N_HEADS = 8
W_ATT = N_HEADS * HEAD_DIM
MOBA_BLOCK = 256
MOBA_TOPK = 3
PAGE_SIZE = 128
ROPE_THETA = 500000.0
ROPE_DIMS = HEAD_DIM // 4
CONV_W = 3
EPS = 1e-6
SCALE = HEAD_DIM ** -0.5
NEG = -1e30

LANES = 128
SUBLANES = 8
BLK_PER_TILE = 2
ATT_TILE = BLK_PER_TILE * MOBA_BLOCK
VMEM_LIMIT = 56 * 1024 * 1024


def _bf(x):
    return x.astype(BF16)


def _dot(a, b):
    return jnp.dot(a, b, preferred_element_type=F32)


def _dot_nt(a, b):
    return lax.dot_general(a, b, (((1,), (1,)), ((), ())), preferred_element_type=F32)


def _split2(x):
    hi = _bf(x)
    lo = _bf(x - hi.astype(F32))
    return hi, lo


def _split3(x):
    hi = _bf(x)
    r = x - hi.astype(F32)
    mid = _bf(r)
    lo = _bf(r - mid.astype(F32))
    return hi, mid, lo


def _dot_left_exact(a_bf, x):
    hi, mid, lo = _split3(x)
    return _dot(a_bf, hi) + _dot(a_bf, mid) + _dot(a_bf, lo)


def _dot_right_exact(x, b_bf):
    hi, mid, lo = _split3(x)
    return _dot(hi, b_bf) + _dot(mid, b_bf) + _dot(lo, b_bf)


def _sigmoid(x):
    return 1.0 / (1.0 + jnp.exp(-x))


def _rms(x):
    return x * lax.rsqrt(jnp.mean(x * x, axis=-1, keepdims=True) + EPS)


def _const_spec(shape):
    nd = len(shape)
    return pl.BlockSpec(shape, lambda *_: (0,) * nd, pipeline_mode=pl.Buffered(1))


def _params(sem):
    return pltpu.CompilerParams(dimension_semantics=sem, vmem_limit_bytes=VMEM_LIMIT)


def _inproj_body(x_ref, g_ref, w_ref, bfl_ref, gains_ref, rope_ref, seg_ref, segt_ref, tri_ref,
                 qa_ref, ka_ref, va_ref, qb_ref, kb_ref, vb_ref, lf_ref, c_ref, sga_ref, sgb_ref,
                 carry_ref, *, tiles_per_seq, kv_transposed):
    t = pl.program_id(0)
    tm = x_ref.shape[0]
    d_model = x_ref.shape[1]
    h = _bf(_rms(x_ref[...]) * g_ref[...])
    seg = seg_ref[...]
    segt = segt_ref[...]
    rope = rope_ref[...]

    def widen(tbl):
        return jnp.concatenate([tbl] * (W_ATT // LANES), axis=1)

    cos_t = widen(rope[:, 0:LANES])
    sin_up = widen(rope[:, LANES:2 * LANES])
    sin_dn = widen(rope[:, 2 * LANES:3 * LANES])

    def proj(c0, n):
        return _dot(h, w_ref[:, c0:c0 + n])

    def headnorm(y, gain):
        ss = _dot(_bf(y * y), seg)
        r = lax.rsqrt(ss * (1.0 / HEAD_DIM) + EPS)
        r_hi, r_lo = _split2(r)
        return y * (_dot(r_hi, segt) + _dot(r_lo, segt)) * gain

    def rope_fn(y):
        half = ROPE_DIMS // 2
        return (y * cos_t + pltpu.roll(y, W_ATT - half, 1) * sin_up
                + pltpu.roll(y, half, 1) * sin_dn)

    def put_kv(ref, y):
        if kv_transposed:
            ref[0] = y.T
        else:
            ref[...] = y

    qa_ref[...] = rope_fn(headnorm(proj(0, W_ATT), gains_ref[0:1, :]))
    put_kv(ka_ref, rope_fn(headnorm(proj(W_ATT, W_ATT), gains_ref[1:2, :])))
    put_kv(va_ref, proj(2 * W_ATT, W_ATT))
    qb_ref[...] = headnorm(proj(3 * W_ATT, W_ATT), gains_ref[2:3, :])
    put_kv(kb_ref, headnorm(proj(4 * W_ATT, W_ATT), gains_ref[3:4, :]))
    put_kv(vb_ref, proj(5 * W_ATT, W_ATT))
    sga_ref[...] = _sigmoid(proj(6 * W_ATT, d_model))
    sgb_ref[...] = _sigmoid(proj(6 * W_ATT + d_model, d_model))

    fl = proj(6 * W_ATT + 2 * d_model, LANES) + bfl_ref[...]
    lf = jnp.minimum(fl, 0.0) - jnp.log1p(jnp.exp(-jnp.abs(fl)))
    lf_ref[...] = lf

    @pl.when(t % tiles_per_seq == 0)
    def _():
        carry_ref[...] = jnp.zeros_like(carry_ref)

    c = _dot_left_exact(tri_ref[...], lf) + carry_ref[0:1, :]
    c_ref[...] = c
    carry_ref[...] = jnp.broadcast_to(c[tm - 1:tm, :], carry_ref.shape)


def _inproj(x, g, w1, bfl, gains, rope, seg, segt, *, tm, tiles_per_seq, kv_transposed):
    m, d_model = x.shape
    n_w = w1.shape[1]
    tri = jnp.asarray(np.tril(np.ones((tm, tm), np.float32)), BF16)
    rope_tiles = rope.shape[0] // tm
    row = lambda n: pl.BlockSpec((tm, n), lambda t: (t, 0))
    tok = jax.ShapeDtypeStruct((m, W_ATT), F32)
    if kv_transposed:
        tps = tiles_per_seq
        kv = jax.ShapeDtypeStruct((m // (tps * tm), W_ATT, tps * tm), F32)
        kv_spec = pl.BlockSpec((1, W_ATT, tm), lambda t: (t // tps, 0, t % tps))
    else:
        kv, kv_spec = tok, row(W_ATT)
    outs = [tok, kv, kv, tok, kv, kv] + [jax.ShapeDtypeStruct((m, LANES), F32)] * 2 \
        + [jax.ShapeDtypeStruct((m, d_model), F32)] * 2
    qkv_specs = [row(W_ATT), kv_spec, kv_spec] * 2
    return pl.pallas_call(
        functools.partial(_inproj_body, tiles_per_seq=tiles_per_seq, kv_transposed=kv_transposed),
        grid=(m // tm,),
        in_specs=[row(d_model), _const_spec((1, d_model)), _const_spec((d_model, n_w)),
                  _const_spec((1, LANES)), _const_spec((SUBLANES, W_ATT)),
                  pl.BlockSpec((tm, 3 * LANES), lambda t: (t % rope_tiles, 0)),
                  _const_spec((W_ATT, LANES)), _const_spec((LANES, W_ATT)), _const_spec((tm, tm))],
        out_specs=qkv_specs + [row(LANES)] * 2 + [row(d_model)] * 2,
        out_shape=outs,
        scratch_shapes=[pltpu.VMEM((SUBLANES, LANES), F32)],
        compiler_params=_params(("arbitrary",)),
        name="inproj",
    )(x, g, w1, bfl, gains, rope, seg, segt, tri)


def _softmax_tile(s, carry, v_t):
    m, l, acc = carry
    m_new = jnp.maximum(m, jnp.max(s, axis=0, keepdims=True))
    alpha = jnp.exp(m - m_new)
    p = jnp.exp(s - m_new)
    l = alpha * l + jnp.sum(p, axis=0, keepdims=True)
    acc = alpha * acc + _dot(v_t, _bf(p))
    return m_new, l, acc


def _softmax_init():
    return (jnp.full((1, 2 * ATT_TILE), NEG, F32), jnp.zeros((1, 2 * ATT_TILE), F32),
            jnp.zeros((LANES, 2 * ATT_TILE), F32))


def _split3_f32(x):
    hi = _bf(x).astype(F32)
    r = x - hi
    mid = _bf(r).astype(F32)
    return hi, mid, _bf(r - mid).astype(F32)


def _head_rows(hh):
    row = lax.broadcasted_iota(jnp.int32, (LANES, 1), 0)
    return (row >= hh * HEAD_DIM) & (row < (hh + 1) * HEAD_DIM)


def _head_lanes(hh):
    lane = lax.broadcasted_iota(jnp.int32, (1, LANES), 1)
    return (lane >= hh * HEAD_DIM) & (lane < (hh + 1) * HEAD_DIM)


def _causal_mask():
    kpos = lax.broadcasted_iota(jnp.int32, (ATT_TILE, ATT_TILE), 0)
    qpos = lax.broadcasted_iota(jnp.int32, (ATT_TILE, ATT_TILE), 1)
    return kpos <= qpos


def _fox_bias_lanes(c, hp):
    lane = lax.broadcasted_iota(jnp.int32, (1, LANES), 1)
    aug = jnp.zeros(c.shape, F32)
    for hh in range(2):
        ck = jnp.sum(jnp.where(lane == 2 * hp + hh, c, 0.0), axis=1, keepdims=True)
        base = SUBLANES * hh
        for off, part in enumerate(_split3_f32(ck)):
            aug = jnp.where(lane == base + off, -part, aug)
        aug = jnp.where((lane >= base + 3) & (lane < base + 6), 1.0, aug)
    return aug


def _fox_bias_rows(cq, hh):
    row = lax.broadcasted_iota(jnp.int32, (LANES, 1), 0)
    base = SUBLANES * hh
    aug = jnp.where((row >= base) & (row < base + 3), 1.0, jnp.zeros((LANES, cq.shape[1]), F32))
    for off, part in enumerate(_split3_f32(cq)):
        aug = jnp.where(row == base + 3 + off, part, aug)
    return aug


def _moba_select(q_ref, km, sel_ref, ns, nb):
    blk = lax.broadcasted_iota(jnp.int32, (nb, 1), 0)
    qlane = lax.broadcasted_iota(jnp.int32, (1, ATT_TILE), 1)
    for j in range(ns):
        own = jnp.zeros((1, ATT_TILE), jnp.int32) + BLK_PER_TILE * j
        for e in range(1, BLK_PER_TILE):
            own = jnp.where(qlane >= e * MOBA_BLOCK, BLK_PER_TILE * j + e, own)
        past = blk < own
        qt_hi, qt_lo = _split2(q_ref[0, j * ATT_TILE:(j + 1) * ATT_TILE, :].T)
        for hh in range(2):
            km_hi, km_lo = _split2(jnp.where(_head_lanes(hh), km, 0.0))
            s = _dot(km_hi, qt_hi) + _dot(km_hi, qt_lo) + _dot(km_lo, qt_hi)
            sm = jnp.where(past, s, -jnp.inf)
            sel = jnp.zeros((nb, ATT_TILE), F32)
            for n in range(BLK_PER_TILE * (j + 1) - 1):
                sn = sm[n:n + 1, :]
                beats = (sm > sn) | ((sm == sn) & (blk < n))
                rank = jnp.sum(beats.astype(F32), axis=0, keepdims=True)
                ok = (rank < MOBA_TOPK) & (jnp.abs(sn) < jnp.inf)
                sel = jnp.where(blk == n, ok.astype(F32), sel)
            sel_ref[hh, j] = jnp.where(blk == own, 1.0, sel)


def _prompt_attn_body(pt_ref, *refs, ns, nb, fox, dec):
    n_in = 5 if fox else 3
    n_sc = 3 if fox else 4
    attn_in, refs = refs[:n_in], refs[n_in:]
    dec_pools, refs = refs[:len(dec.pools)], refs[len(dec.pools):]
    dec_in, refs = refs[:len(dec.in_specs)], refs[len(dec.in_specs):]
    (o_ref, dec_o_ref), refs = refs[:2], refs[2:]
    attn_sc, dec_sc = refs[:n_sc], refs[n_sc:]
    assert len(dec_sc) == len(dec.scratch)
    if fox:
        q_ref, k_ref, v_ref, ct_ref, c_ref = attn_in
        qtm_ref, vt_ref, kb_ref = attn_sc
    else:
        q_ref, k_ref, v_ref = attn_in
        qtm_ref, vt_ref, kb_ref, sel_ref = attn_sc
    hp = pl.program_id(1)
    i = pl.program_id(2)
    step = (pl.program_id(0) * pl.num_programs(1) + hp) * ns + i
    n_steps = pl.num_programs(0) * pl.num_programs(1) * ns
    dec_j = step % dec.steps_per_seq

    @pl.when(i == 0)
    def _():
        blk = lax.broadcasted_iota(jnp.int32, (nb, 1), 0)
        km = jnp.zeros((nb, LANES), F32)
        for j in range(ns):
            rows = slice(j * ATT_TILE, (j + 1) * ATT_TILE)
            kf = k_ref[0, :, rows].T
            kb_ref[rows, 0:LANES] = _bf(kf)
            vt_ref[j] = _bf(v_ref[0, :, rows])
            if not fox:
                for e in range(BLK_PER_TILE):
                    ksum = jnp.sum(kf[e * MOBA_BLOCK:(e + 1) * MOBA_BLOCK, :], axis=0, keepdims=True)
                    km = jnp.where(blk == BLK_PER_TILE * j + e, ksum * (1.0 / MOBA_BLOCK), km)
        if fox:
            kb_ref[:, LANES:2 * LANES] = _bf(_fox_bias_lanes(c_ref[0], hp))
        else:
            _moba_select(q_ref, km, sel_ref, ns, nb)
        for j in range(ns):
            rows = slice(j * ATT_TILE, (j + 1) * ATT_TILE)
            qt = (q_ref[0, rows, :] * SCALE).T
            for hh in range(2):
                cols = slice(hh * ATT_TILE, (hh + 1) * ATT_TILE)
                qtm_ref[j, 0:LANES, cols] = _bf(jnp.where(_head_rows(hh), qt, 0.0))
                if fox:
                    cq = ct_ref[0, pl.ds(2 * hp + hh, 1), rows]
                    qtm_ref[j, LANES:2 * LANES, cols] = _bf(_fox_bias_rows(cq, hh))

    causal = jnp.concatenate([_causal_mask()] * 2, axis=1)

    def tile(n, carry, diag):
        rows = pl.ds(pl.multiple_of(n * ATT_TILE, ATT_TILE), ATT_TILE)
        s = _dot(kb_ref[rows, :], qtm_ref[i])
        if fox:
            keep = causal if diag else None
        else:
            keep = jnp.concatenate(
                [jnp.concatenate([jnp.broadcast_to(sel_ref[hh, i, pl.ds(BLK_PER_TILE * n + e, 1), :],
                                                   (MOBA_BLOCK, ATT_TILE)) for hh in range(2)], axis=1)
                 for e in range(BLK_PER_TILE)], axis=0) > 0.5
            if diag:
                keep = keep & causal
        if keep is not None:
            s = jnp.where(keep, s, NEG)
        return _softmax_tile(s, carry, vt_ref[n])

    dec.main(pt_ref, step, n_steps, dec_pools, dec_in, dec_sc)
    carry = tile(i, _softmax_init(), True)
    _, l, acc = lax.fori_loop(0, i, lambda n, c: tile(n, c, False), carry)
    out = acc / l
    o_ref[0] = jnp.concatenate([out[hh * HEAD_DIM:(hh + 1) * HEAD_DIM, hh * ATT_TILE:(hh + 1) * ATT_TILE]
                                for hh in range(2)], axis=0).T

    @pl.when(dec_j == dec.steps_per_seq - 1)
    def _():
        dec.final(dec_in, dec_o_ref, dec_sc)


class _DecodeStream(NamedTuple):
    steps_per_seq: int
    pools: tuple
    in_specs: list
    args: tuple
    out_spec: pl.BlockSpec
    out_shape: jax.ShapeDtypeStruct
    scratch: list
    main: Callable
    final: Callable


def _page_copies(pt_ref, step, slot, pools, bufs, sems, *, spd, pg, last_first):
    n_pages = pt_ref.shape[1]
    seq, j = step // spd, step % spd
    copies = []
    for i in range(pg):
        col = n_pages - 1 - (j * pg + i) if last_first else j * pg + i
        page = pt_ref[seq, col]
        for pool, buf, sem in zip(pools, bufs, sems):
            copies.append(pltpu.make_async_copy(pool.at[page], buf.at[slot, i], sem.at[slot]))
    return copies


def _double_buffer(step, n_steps, copies):
    slot = step % 2

    @pl.when(step == 0)
    def _():
        for c in copies(step, slot):
            c.start()

    @pl.when(step + 1 < n_steps)
    def _():
        for c in copies(step + 1, 1 - slot):
            c.start()

    for c in copies(step, slot):
        c.wait()
    return slot


def _stream_pages(pt_ref, step, n_steps, pools, bufs, sems, **kw):
    return _double_buffer(step, n_steps, lambda s, half: _page_copies(pt_ref, s, half, pools, bufs, sems, **kw))


def _prompt_attention(page_table, dec, q, k, v, c=None, c_t=None):
    b, t_len, _ = q.shape
    ns = t_len // ATT_TILE
    nb = t_len // MOBA_BLOCK
    n_hp = W_ATT // LANES
    fox = c is not None
    kdim = 2 * LANES if fox else LANES
    spd = dec.steps_per_seq
    assert b * n_hp * ns == page_table.shape[0] * spd

    def dec_map(imap):
        def wrapped(bi, hp, i, pt):
            step = (bi * n_hp + hp) * ns + i
            return imap(step // spd, step % spd, pt)
        return wrapped

    dec_in_specs = [pl.BlockSpec(memory_space=pl.ANY)] * len(dec.pools)
    dec_in_specs += [pl.BlockSpec(s.block_shape, dec_map(s.index_map)) for s in dec.in_specs]
    dec_out_spec = pl.BlockSpec(dec.out_spec.block_shape, dec_map(dec.out_spec.index_map))
    q_spec = pl.BlockSpec((1, t_len, LANES), lambda bi, hp, i, pt: (bi, 0, hp))
    kv_spec = pl.BlockSpec((1, LANES, t_len), lambda bi, hp, i, pt: (bi, hp, 0))
    in_specs = [q_spec, kv_spec, kv_spec]
    scratch = [pltpu.VMEM((ns, kdim, 2 * ATT_TILE), BF16), pltpu.VMEM((ns, LANES, ATT_TILE), BF16),
               pltpu.VMEM((t_len, kdim), BF16)]
    if fox:
        in_specs += [pl.BlockSpec((1, N_HEADS, t_len), lambda bi, hp, i, pt: (bi, 0, 0)),
                     pl.BlockSpec((1, t_len, LANES), lambda bi, hp, i, pt: (bi, 0, 0))]
        args = (q, k, v, c_t, c)
    else:
        args = (q, k, v)
        scratch.append(pltpu.VMEM((2, ns, nb, ATT_TILE), F32))
    grid_spec = pltpu.PrefetchScalarGridSpec(
        num_scalar_prefetch=1,
        grid=(b, n_hp, ns),
        in_specs=in_specs + dec_in_specs,
        out_specs=[pl.BlockSpec((1, ATT_TILE, LANES), lambda bi, hp, i, pt: (bi, i, hp)), dec_out_spec],
        scratch_shapes=scratch + list(dec.scratch),
    )
    return pl.pallas_call(
        functools.partial(_prompt_attn_body, ns=ns, nb=nb, fox=fox, dec=dec),
        grid_spec=grid_spec,
        out_shape=[jax.ShapeDtypeStruct((b, t_len, W_ATT), F32), dec.out_shape],
        compiler_params=_params(("arbitrary", "arbitrary", "arbitrary")),
        name="fox_prompt_decode" if fox else "moba_prompt_scores",
    )(page_table, *args, *dec.pools, *dec.args)


def _ffn_body(*refs, decode, ck, rider_hps):
    if not decode:
        (pt_ref, top_ref), refs = refs[:2], refs[2:]
    x_ref, ya_ref, yb_ref, sga_ref, sgb_ref, p_ref = refs[:6]
    refs = refs[6:]
    if decode:
        buf0_ref, buf1_ref = refs[:2]
        refs = refs[2:]
    (wbm_ref, wbf_ref, wo_ref, g_ref, wup_ref, wgate_ref, cw_ref, cb_ref, wdown_ref,
     gp_ref, wpg_ref, wple_ref), refs = refs[:12], refs[12:]
    if decode:
        o_ref, tail_ref = refs
    else:
        pool_k_ref, pool_v_ref, rq_ref, rk_ref, rv_ref, o_ref, tail_ref, ro_ref = refs[:8]
        carry_ref, kbuf, vbuf, ksem, vsem = refs[8:]

        @pl.when(pl.program_id(1) == 0)
        def _():
            carry_ref[...] = jnp.zeros_like(carry_ref)

        step = pl.program_id(0) * pl.num_programs(1) + pl.program_id(1)
        _moba_decode_step(pt_ref, top_ref, step, pl.num_programs(0) * pl.num_programs(1),
                          (pool_k_ref, pool_v_ref), rq_ref, rk_ref, rv_ref, ro_ref,
                          (kbuf, vbuf), (ksem, vsem), hps=rider_hps)

    merged = (sga_ref[...] * _dot(_bf(ya_ref[...]), wbm_ref[...])
              + sgb_ref[...] * _dot(_bf(yb_ref[...]), wbf_ref[...]))
    x = x_ref[...] + _dot(_bf(merged), wo_ref[...])
    tm = x.shape[0]
    d_ff = wup_ref.shape[1]
    h = _bf(_rms(x) * g_ref[...])
    acc = jnp.zeros_like(x)
    for c in range(d_ff // ck):
        cols = slice(c * ck, (c + 1) * ck)
        u = _dot(h, wup_ref[:, cols])
        a = _dot(h, wgate_ref[:, cols])
        if decode:
            prev2 = buf0_ref[:, cols]
            prev1 = buf1_ref[:, cols]
            tail_ref[:, cols] = a
        else:
            ext = jnp.concatenate([carry_ref[:, cols], a], axis=0)
            prev1 = pltpu.roll(ext, 1, 0)[SUBLANES:, :]
            prev2 = pltpu.roll(ext, 2, 0)[SUBLANES:, :]
            carry_ref[:, cols] = a[tm - SUBLANES:, :]
            tail_ref[0, :, cols] = a[tm - SUBLANES:, :]
        conv = cb_ref[:, cols] + cw_ref[0:1, cols] * prev2 + cw_ref[1:2, cols] * prev1 + cw_ref[2:3, cols] * a
        act = conv * _sigmoid(conv) * u
        acc = acc + _dot(_bf(act), wdown_ref[cols, :])
    x = x + acc
    gate = _sigmoid(_dot(_bf(_rms(x) * gp_ref[...]), wpg_ref[...]))
    o_ref[...] = x + _dot(_bf(p_ref[...]), wple_ref[...]) * gate


def _ffn(x, ya, yb, sga, sgb, p, bufs, weights, *, n_seq, tm, ck, rider=None):
    m, d_model = x.shape
    d_ff = weights[4].shape[1]
    decode = bufs is not None
    w_specs = [_const_spec(w.shape) for w in weights]
    acts = (x, ya, yb, sga, sgb, p)
    if decode:
        row = lambda n: pl.BlockSpec((tm, n), lambda t: (t, 0))
        return pl.pallas_call(
            functools.partial(_ffn_body, decode=True, ck=ck, rider_hps=None),
            grid=(m // tm,),
            in_specs=[row(a.shape[1]) for a in acts] + [row(d_ff), row(d_ff)] + w_specs,
            out_specs=[row(d_model), row(d_ff)],
            out_shape=[jax.ShapeDtypeStruct((m, d_model), F32), jax.ShapeDtypeStruct((m, d_ff), F32)],
            compiler_params=_params(("arbitrary",)),
            name="ffn_decode",
        )(*acts, *bufs, *weights)
    tps = m // n_seq // tm
    hps = rider.hps
    gps = N_HEADS // hps
    db = rider.page_table.shape[0]
    assert n_seq * tps == db * gps
    row = lambda n: pl.BlockSpec((tm, n), lambda s, t, pt, top: (s * tps + t, 0))
    vec_spec = pl.BlockSpec((1, hps * HEAD_DIM, 1),
                            lambda s, t, pt, top: ((s * tps + t) // gps, (s * tps + t) % gps, 0))
    page_buf = pltpu.VMEM((2, hps * N_PG_MOBA, HEAD_DIM, PAGE_SIZE), F32)
    grid_spec = pltpu.PrefetchScalarGridSpec(
        num_scalar_prefetch=2,
        grid=(n_seq, tps),
        in_specs=[row(a.shape[1]) for a in acts] + w_specs + [pl.BlockSpec(memory_space=pl.ANY)] * 2
        + [vec_spec] * 3,
        out_specs=[row(d_model), pl.BlockSpec((1, SUBLANES, d_ff), lambda s, t, pt, top: (s, 0, 0)), vec_spec],
        scratch_shapes=[pltpu.VMEM((SUBLANES, d_ff), F32), page_buf, page_buf,
                        pltpu.SemaphoreType.DMA((2,)), pltpu.SemaphoreType.DMA((2,))],
    )
    return pl.pallas_call(
        functools.partial(_ffn_body, decode=False, ck=ck, rider_hps=hps),
        grid_spec=grid_spec,
        out_shape=[jax.ShapeDtypeStruct((m, d_model), F32), jax.ShapeDtypeStruct((n_seq, SUBLANES, d_ff), F32),
                   jax.ShapeDtypeStruct((db, W_ATT, 1), F32)],
        compiler_params=_params(("arbitrary", "arbitrary")),
        name="ffn_prompt_moba_decode",
    )(rider.page_table, rider.top_flat, *acts, *weights, rider.pool_k, rider.pool_v,
      rider.q_col, rider.k_col, rider.v_col)


def _moba_score_main(pt_ref, step, n_steps, pools, in_refs, sc_refs, *, spd, pg):
    q_ref, = in_refs
    sc_ref, kbuf, sem = sc_refs
    slot = _stream_pages(pt_ref, step, n_steps, pools, (kbuf,), (sem,), spd=spd, pg=pg, last_first=False)
    j = step % spd
    bps = pg // 2
    lane = lax.broadcasted_iota(jnp.int32, (1, LANES), 1)

    @pl.when(j == 0)
    def _():
        sc_ref[...] = jnp.full_like(sc_ref, -jnp.inf)

    qb = jnp.broadcast_to(q_ref[0], (W_ATT, PAGE_SIZE))
    sc = sc_ref[...]
    for m in range(bps):
        t = (kbuf[slot, 2 * m] + kbuf[slot, 2 * m + 1]) * qb
        per_pos = jnp.sum(t.reshape(N_HEADS, HEAD_DIM, PAGE_SIZE), axis=1)
        score = jnp.sum(per_pos, axis=1, keepdims=True) * (1.0 / MOBA_BLOCK)
        sc = jnp.where(lane == j * bps + m, score, sc)
    sc_ref[...] = sc


def _moba_score_final(in_refs, top_ref, sc_refs):
    del in_refs
    lane = lax.broadcasted_iota(jnp.int32, (1, LANES), 1)
    s = sc_refs[0][...]
    idx = lane.astype(F32)
    top = jnp.zeros((N_HEADS, LANES), F32)
    for r in range(MOBA_TOPK):
        best = jnp.max(s, axis=1, keepdims=True)
        arg = jnp.min(jnp.where(s == best, idx, float(LANES)), axis=1, keepdims=True)
        top = jnp.where(lane == r, arg, top)
        s = jnp.where(idx == arg, -jnp.inf, s)
    top_ref[0] = top.astype(jnp.int32)


def _moba_score_stream(page_table, pool_k, q_col, *, pg):
    db, n_pages = page_table.shape
    assert MOBA_TOPK <= n_pages // 2 <= LANES and pg % 2 == 0 and n_pages % pg == 0
    spd = n_pages // pg
    return _DecodeStream(
        steps_per_seq=spd,
        pools=(pool_k,),
        in_specs=[pl.BlockSpec((1, W_ATT, 1), lambda b, j, pt: (b, 0, 0))],
        args=(q_col,),
        out_spec=pl.BlockSpec((1, N_HEADS, LANES), lambda b, j, pt: (b, 0, 0)),
        out_shape=jax.ShapeDtypeStruct((db, N_HEADS, LANES), jnp.int32),
        scratch=[pltpu.VMEM((N_HEADS, LANES), F32), pltpu.VMEM((2, pg, W_ATT, PAGE_SIZE), F32),
                 pltpu.SemaphoreType.DMA((2,))],
        main=functools.partial(_moba_score_main, spd=spd, pg=pg),
        final=_moba_score_final,
    )


class _MobaDecodeRider(NamedTuple):
    page_table: jax.Array
    top_flat: jax.Array
    pool_k: jax.Array
    pool_v: jax.Array
    q_col: jax.Array
    k_col: jax.Array
    v_col: jax.Array
    hps: int


N_PG_MOBA = 2 * MOBA_TOPK


def _moba_decode_copies(pt_ref, top_ref, step, slot, pools, bufs, sems, *, hps):
    gps = N_HEADS // hps
    seq, g = step // gps, step % gps
    copies = []
    for hl in range(hps):
        h = g * hps + hl
        rows = pl.ds(pl.multiple_of(h * HEAD_DIM, HEAD_DIM), HEAD_DIM)
        for r in range(MOBA_TOPK):
            blk = top_ref[(seq * N_HEADS + h) * MOBA_TOPK + r]
            for e in range(2):
                page = pt_ref[seq, 2 * blk + e]
                i = (hl * MOBA_TOPK + r) * 2 + e
                for pool, buf, sem in zip(pools, bufs, sems):
                    copies.append(pltpu.make_async_copy(pool.at[page, rows], buf.at[slot, i], sem.at[slot]))
    return copies


def _moba_decode_step(pt_ref, top_ref, step, n_steps, pools, q_ref, kn_ref, vn_ref, o_ref, bufs, sems, *, hps):
    kbuf, vbuf = bufs
    slot = _double_buffer(step, n_steps, lambda s, half: _moba_decode_copies(
        pt_ref, top_ref, s, half, pools, bufs, sems, hps=hps))
    for hl in range(hps):
        dims = slice(hl * HEAD_DIM, (hl + 1) * HEAD_DIM)
        pages = range(hl * N_PG_MOBA, (hl + 1) * N_PG_MOBA)
        q = q_ref[0, dims, :] * SCALE
        logits = [jnp.sum(kbuf[slot, i] * q, axis=0, keepdims=True) for i in pages]
        l_new = jnp.sum(q * kn_ref[0, dims, :], axis=0, keepdims=True)
        s_max = logits[0]
        for s in logits[1:]:
            s_max = jnp.maximum(s_max, s)
        m = jnp.maximum(l_new, jnp.max(s_max, axis=1, keepdims=True))
        p_new = jnp.exp(l_new - m)
        p_sum = jnp.zeros((1, PAGE_SIZE), F32)
        pv = jnp.zeros((HEAD_DIM, PAGE_SIZE), F32)
        for i, s in zip(pages, logits):
            p = jnp.exp(s - m)
            p_sum = p_sum + p
            pv = pv + vbuf[slot, i] * p
        denom = p_new + jnp.sum(p_sum, axis=1, keepdims=True)
        acc = p_new * vn_ref[0, dims, :] + jnp.sum(pv, axis=1, keepdims=True)
        o_ref[0, dims, :] = acc / denom


def _fox_q8(q_ref):
    head = lax.broadcasted_iota(jnp.int32, (N_HEADS, 1), 0)
    lane = lax.broadcasted_iota(jnp.int32, (1, W_ATT), 1)
    own = (lane >= head * HEAD_DIM) & (lane < (head + 1) * HEAD_DIM)
    return jnp.where(own, q_ref[0] * SCALE, 0.0), own


def _fox_decode_main(pt_ref, step, n_steps, pools, in_refs, sc_refs, *, spd, pg):
    q_ref, _, _, cn_ref, u_ref = in_refs
    m_ref, l_ref, acc_ref, sfx_ref, kbuf, vbuf, lfbuf, ksem, vsem, lfsem = sc_refs
    slot = _stream_pages(pt_ref, step, n_steps, pools, (kbuf, vbuf, lfbuf), (ksem, vsem, lfsem),
                         spd=spd, pg=pg, last_first=True)

    @pl.when(step % spd == 0)
    def _():
        m_ref[...] = jnp.full_like(m_ref, NEG)
        l_ref[...] = jnp.zeros_like(l_ref)
        acc_ref[...] = jnp.zeros_like(acc_ref)
        sfx_ref[...] = jnp.zeros_like(sfx_ref)

    q8b = _bf(_fox_q8(q_ref)[0])
    cn = cn_ref[0]
    m = m_ref[:, 0:1]
    l = l_ref[:, 0:1]
    sfx = sfx_ref[...]
    lfts = [lfbuf[slot, i] for i in range(pg)]
    within = _dot_right_exact(jnp.concatenate(lfts, axis=0), u_ref[...])
    logits = []
    for i in range(pg):
        r = within[i * N_HEADS:(i + 1) * N_HEADS, :] + sfx
        logits.append(_dot(q8b, _bf(kbuf[slot, i])) + cn + r)
        sfx = sfx + jnp.sum(lfts[i], axis=1, keepdims=True)
    s_max = logits[0]
    for s in logits[1:]:
        s_max = jnp.maximum(s_max, s)
    m_new = jnp.maximum(m, jnp.max(s_max, axis=1, keepdims=True))
    alpha = jnp.exp(m - m_new)
    p_sum = jnp.zeros((N_HEADS, PAGE_SIZE), F32)
    pv = jnp.zeros((N_HEADS, W_ATT), F32)
    for i, s in enumerate(logits):
        p = jnp.exp(s - m_new)
        p_sum = p_sum + p
        pv = pv + _dot_nt(_bf(p), _bf(vbuf[slot, i]))
    l = alpha * l + jnp.sum(p_sum, axis=1, keepdims=True)
    acc = alpha * acc_ref[...] + pv
    m = m_new
    m_ref[...] = jnp.broadcast_to(m, m_ref.shape)
    l_ref[...] = jnp.broadcast_to(l, l_ref.shape)
    acc_ref[...] = acc
    sfx_ref[...] = sfx


def _fox_decode_final(in_refs, o_ref, sc_refs):
    q_ref, kn_ref, vn_ref, cn_ref, _ = in_refs[-5:]
    m_ref, l_ref, acc_ref = sc_refs[:3]
    q8, own = _fox_q8(q_ref)
    m = m_ref[:, 0:1]
    cn1 = cn_ref[0][:, 0:1]
    l_new = jnp.sum(q8 * kn_ref[0], axis=1, keepdims=True) + cn1 - cn1
    m_fin = jnp.maximum(m, l_new)
    a = jnp.exp(m - m_fin)
    p_new = jnp.exp(l_new - m_fin)
    o8 = (a * acc_ref[...] + p_new * vn_ref[0]) / (a * l_ref[:, 0:1] + p_new)
    o_ref[0] = jnp.sum(jnp.where(own, o8, 0.0), axis=0, keepdims=True)


def _fox_decode_stream(page_table, pool_k, pool_v, pool_lft, q, k_new, v_new, cn, *, pg):
    db, n_pages = page_table.shape
    assert n_pages % pg == 0

    spd = n_pages // pg
    vec_spec = pl.BlockSpec((1, 1, W_ATT), lambda b, j, pt: (b, 0, 0))
    u = jnp.asarray(np.tril(np.ones((PAGE_SIZE, PAGE_SIZE), np.float32), -1), BF16)
    page_buf = pltpu.VMEM((2, pg, W_ATT, PAGE_SIZE), F32)
    return _DecodeStream(
        steps_per_seq=spd,
        pools=(pool_k, pool_v, pool_lft),
        in_specs=[vec_spec, vec_spec, vec_spec,
                  pl.BlockSpec((1, N_HEADS, LANES), lambda b, j, pt: (b, 0, 0)),
                  pl.BlockSpec((PAGE_SIZE, PAGE_SIZE), lambda b, j, pt: (0, 0))],
        args=(q, k_new, v_new, cn, u),
        out_spec=vec_spec,
        out_shape=jax.ShapeDtypeStruct((db, 1, W_ATT), F32),
        scratch=[pltpu.VMEM((N_HEADS, LANES), F32), pltpu.VMEM((N_HEADS, LANES), F32),
                 pltpu.VMEM((N_HEADS, W_ATT), F32), pltpu.VMEM((N_HEADS, LANES), F32),
                 page_buf, page_buf, pltpu.VMEM((2, pg, N_HEADS, PAGE_SIZE), F32)]
        + [pltpu.SemaphoreType.DMA((2,))] * 3,
        main=functools.partial(_fox_decode_main, spd=spd, pg=pg),
        final=_fox_decode_final,
    )


def _rope_table(pos):
    half = ROPE_DIMS // 2
    inv = ROPE_THETA ** (-jnp.arange(0, ROPE_DIMS, 2, dtype=F32) / ROPE_DIMS)
    ang = pos.astype(F32)[:, None] * inv[None, :]
    d = np.arange(LANES) % HEAD_DIM
    sel = jnp.asarray(d % half)
    cos = jnp.where(jnp.asarray(d < ROPE_DIMS)[None, :], jnp.cos(ang)[:, sel], 1.0)
    sin = jnp.sin(ang)[:, sel]
    sin_up = jnp.where(jnp.asarray(d < half)[None, :], -sin, 0.0)
    sin_dn = jnp.where(jnp.asarray((d >= half) & (d < ROPE_DIMS))[None, :], sin, 0.0)
    return jnp.concatenate([cos, sin_up, sin_dn], axis=1)


def kernel(x_prompt, x_sample, cache_moba_k, cache_moba_v, cache_fox_k, cache_fox_v, cache_fox_logf,
           state_ffn_conv, page_table, p_prompt, p_sample, norm_attn_g, w_in, b_forget,
           qnorm_moba, knorm_moba, qnorm_fox, knorm_fox, w_branch_moba, w_branch_fox, w_out,
           norm_ffn_g, w_up, w_gate, conv_w, conv_b, w_down, norm_ple_g, w_ple, w_ple_gate):
    b, t_len, d_model = x_prompt.shape
    db, ts, _ = x_sample.shape
    depth = w_in.shape[0]
    n_pages = page_table.shape[1]
    n_pool = cache_moba_k.shape[1]
    assert depth == 1 and ts == 1 and t_len % ATT_TILE == 0 and n_pages % 2 == 0
    past_len = n_pages * PAGE_SIZE
    d_ff = w_up.shape[2]
    n_att = 6 * W_ATT

    w_in0 = w_in[0]
    w_fl = jnp.pad(w_in0[:, n_att:n_att + N_HEADS], ((0, 0), (0, LANES - N_HEADS)))
    w1 = _bf(jnp.concatenate([w_in0[:, :n_att], w_in0[:, n_att + N_HEADS:], w_fl], axis=1))
    bfl = jnp.pad(b_forget[0], (0, LANES - N_HEADS))[None, :]
    gains = jnp.stack([jnp.tile(g[0], N_HEADS) for g in (qnorm_moba, knorm_moba, qnorm_fox, knorm_fox)])
    gains = jnp.pad(gains, ((0, SUBLANES - 4), (0, 0)))
    seg_np = (np.arange(W_ATT)[:, None] // HEAD_DIM == np.arange(LANES)[None, :]).astype(np.float32)
    seg = jnp.asarray(seg_np, BF16)
    segt = jnp.asarray(seg_np.T, BF16)
    g_attn = norm_attn_g[0][None, :]
    ffn_w = (_bf(w_branch_moba[0]), _bf(w_branch_fox[0]), _bf(w_out[0]),
             norm_ffn_g[0][None, :], _bf(w_up[0]), _bf(w_gate[0]),
             jnp.pad(conv_w[0], ((0, SUBLANES - CONV_W), (0, 0))), conv_b[0][None, :], _bf(w_down[0]),
             norm_ple_g[0][None, :], _bf(w_ple_gate[0]), _bf(w_ple[0]))

    tm = 256
    xp = x_prompt.reshape(b * t_len, d_model)
    rope_p = _rope_table(jnp.arange(t_len, dtype=jnp.int32))
    qa, ka, va, qb, kb, vb, lf, c, sga, sgb = _inproj(
        xp, g_attn, w1, bfl, gains, rope_p, seg, segt, tm=tm, tiles_per_seq=t_len // tm, kv_transposed=True)
    xs = x_sample.reshape(db, d_model)
    rope_s = _rope_table(jnp.full((db,), past_len, jnp.int32))
    qa_s, ka_s, va_s, qb_s, kb_s, vb_s, lf_s, _, sga_s, sgb_s = _inproj(
        xs, g_attn, w1, bfl, gains, rope_s, seg, segt, tm=db, tiles_per_seq=1, kv_transposed=False)

    attn_steps = b * (W_ATT // LANES) * (t_len // ATT_TILE)
    assert attn_steps % db == 0 and n_pages % (attn_steps // db) == 0
    pg = n_pages // (attn_steps // db)
    v3 = lambda a: a.reshape(db, 1, W_ATT)
    col = lambda a: a.reshape(db, W_ATT, 1)
    pool = lambda cache: cache[0].transpose(0, 2, 3, 1).reshape(n_pool, W_ATT, PAGE_SIZE)
    pool_mk = pool(cache_moba_k)
    r3 = lambda a: a.reshape(b, t_len, a.shape[-1])
    ya, top = _prompt_attention(page_table, _moba_score_stream(page_table, pool_mk, col(qa_s), pg=pg),
                                r3(qa), ka, va)
    c3 = r3(c)
    c_t = c3[:, :, :N_HEADS].transpose(0, 2, 1)
    cn = jnp.broadcast_to(lf_s[:, :N_HEADS, None], (db, N_HEADS, LANES))
    fox_stream = _fox_decode_stream(page_table, pool(cache_fox_k), pool(cache_fox_v),
                                    cache_fox_logf[0].transpose(0, 2, 1), v3(qb_s), v3(kb_s), v3(vb_s), cn, pg=pg)
    yb, yb_s = _prompt_attention(page_table, fox_stream, r3(qb), kb, vb, c3, c_t)
    top_flat = top[:, :, :MOBA_TOPK].reshape(-1)

    mlp_steps = b * t_len // tm
    assert (N_HEADS * db) % mlp_steps == 0
    rider = _MobaDecodeRider(page_table, top_flat, pool_mk, pool(cache_moba_v), col(qa_s), col(ka_s), col(va_s),
                             hps=N_HEADS * db // mlp_steps)
    y_p, tail_p, ya_s = _ffn(xp, ya.reshape(b * t_len, W_ATT), yb.reshape(b * t_len, W_ATT), sga, sgb,
                             p_prompt[0].reshape(b * t_len, -1), None, ffn_w, n_seq=b, tm=tm, ck=512, rider=rider)
    bufs = (state_ffn_conv[0, :, 0, :], state_ffn_conv[0, :, 1, :])
    y_s, a_s = _ffn(xs, ya_s.reshape(db, W_ATT), yb_s.reshape(db, W_ATT), sga_s, sgb_s,
                    p_sample[0].reshape(db, -1), bufs, ffn_w, n_seq=db, tm=db, ck=512)

    heads_t = lambda a: a.reshape(1, b, N_HEADS, HEAD_DIM, t_len).transpose(0, 1, 4, 2, 3)
    heads = lambda a: a.reshape(1, db, ts, N_HEADS, HEAD_DIM)
    outs_p = (heads_t(ka), heads_t(va), heads_t(kb), heads_t(vb),
              lf[:, :N_HEADS].reshape(1, b, t_len, N_HEADS), tail_p[None, :, SUBLANES - (CONV_W - 1):, :])
    conv_s = jnp.stack([state_ffn_conv[0, :, 1, :], a_s], axis=1)[None]
    outs_s = (heads(ka_s), heads(va_s), heads(kb_s), heads(vb_s),
              lf_s[:, :N_HEADS].reshape(1, db, ts, N_HEADS), conv_s)
    return (y_p.reshape(b, t_len, d_model), y_s.reshape(db, ts, d_model)) + outs_p + outs_s
```

```python
import functools
from typing import Callable, NamedTuple

import jax
import jax.numpy as jnp
import numpy as np
from jax import lax
from jax.experimental import pallas as pl
from jax.experimental.pallas import tpu as pltpu

F32 = jnp.float32
BF16 = jnp.bfloat16

HEAD_DIM = 64
N_HEADS = 8
W_ATT = N_HEADS * HEAD_DIM
MOBA_BLOCK = 256
MOBA_TOPK = 3
PAGE_SIZE = 128
ROPE_THETA = 500000.0
ROPE_DIMS = HEAD_DIM // 4
CONV_W = 3
EPS = 1e-6
SCALE = HEAD_DIM ** -0.5
NEG = -1e30

LANES = 128
SUBLANES = 8
BLK_PER_TILE = 2
ATT_TILE = BLK_PER_TILE * MOBA_BLOCK
VMEM_LIMIT = 56 * 1024 * 1024


def _bf(x):
    return x.astype(BF16)


def _dot(a, b):
    return jnp.dot(a, b, preferred_element_type=F32)


def _dot_nt(a, b):
    return lax.dot_general(a, b, (((1,), (1,)), ((), ())), preferred_element_type=F32)


def _split2(x):
    hi = _bf(x)
    lo = _bf(x - hi.astype(F32))
    return hi, lo


def _split3(x):
    hi = _bf(x)
    r = x - hi.astype(F32)
    mid = _bf(r)
    lo = _bf(r - mid.astype(F32))
    return hi, mid, lo


def _dot_left_exact(a_bf, x):
    hi, mid, lo = _split3(x)
    return _dot(a_bf, hi) + _dot(a_bf, mid) + _dot(a_bf, lo)


def _dot_right_exact(x, b_bf):
    hi, mid, lo = _split3(x)
    return _dot(hi, b_bf) + _dot(mid, b_bf) + _dot(lo, b_bf)


def _sigmoid(x):
    return 1.0 / (1.0 + jnp.exp(-x))


def _rms(x):
    return x * lax.rsqrt(jnp.mean(x * x, axis=-1, keepdims=True) + EPS)


def _const_spec(shape):
    nd = len(shape)
    return pl.BlockSpec(shape, lambda *_: (0,) * nd, pipeline_mode=pl.Buffered(1))


def _params(sem):
    return pltpu.CompilerParams(dimension_semantics=sem, vmem_limit_bytes=VMEM_LIMIT)


def _inproj_body(x_ref, g_ref, w_ref, bfl_ref, gains_ref, rope_ref, seg_ref, segt_ref, tri_ref,
                 qa_ref, ka_ref, va_ref, qb_ref, kb_ref, vb_ref, lf_ref, c_ref, sga_ref, sgb_ref,
                 carry_ref, *, tiles_per_seq, kv_transposed):
    t = pl.program_id(0)
    tm = x_ref.shape[0]
    d_model = x_ref.shape[1]
    h = _bf(_rms(x_ref[...]) * g_ref[...])
    seg = seg_ref[...]
    segt = segt_ref[...]
    rope = rope_ref[...]

    def widen(tbl):
        return jnp.concatenate([tbl] * (W_ATT // LANES), axis=1)

    cos_t = widen(rope[:, 0:LANES])
    sin_up = widen(rope[:, LANES:2 * LANES])
    sin_dn = widen(rope[:, 2 * LANES:3 * LANES])

    att = _dot(h, w_ref[:, 0:6 * W_ATT])

    def proj(c0, n):
        if c0 + n <= 6 * W_ATT:
            return att[:, c0:c0 + n]
        return _dot(h, w_ref[:, c0:c0 + n])

    def headnorm(y, gain):
        ss = _dot(_bf(y * y), seg)
        r = lax.rsqrt(ss * (1.0 / HEAD_DIM) + EPS)
        r_hi, r_lo = _split2(r)
        return y * (_dot(r_hi, segt) + _dot(r_lo, segt)) * gain

    def rope_fn(y):
        half = ROPE_DIMS // 2
        return (y * cos_t + pltpu.roll(y, W_ATT - half, 1) * sin_up
                + pltpu.roll(y, half, 1) * sin_dn)

    def put_kv(ref, y):
        if kv_transposed:
            ref[0] = y.T
        else:
            ref[...] = y

    qa_ref[...] = rope_fn(headnorm(proj(0, W_ATT), gains_ref[0:1, :]))
    put_kv(ka_ref, rope_fn(headnorm(proj(W_ATT, W_ATT), gains_ref[1:2, :])))
    put_kv(va_ref, proj(2 * W_ATT, W_ATT))
    qb_ref[...] = headnorm(proj(3 * W_ATT, W_ATT), gains_ref[2:3, :])
    put_kv(kb_ref, headnorm(proj(4 * W_ATT, W_ATT), gains_ref[3:4, :]))
    put_kv(vb_ref, proj(5 * W_ATT, W_ATT))
    sga_ref[...] = _sigmoid(proj(6 * W_ATT, d_model))
    sgb_ref[...] = _sigmoid(proj(6 * W_ATT + d_model, d_model))

    fl = proj(6 * W_ATT + 2 * d_model, LANES) + bfl_ref[...]
    lf = jnp.minimum(fl, 0.0) - jnp.log1p(jnp.exp(-jnp.abs(fl)))
    lf_ref[...] = lf

    @pl.when(t % tiles_per_seq == 0)
    def _():
        carry_ref[...] = jnp.zeros_like(carry_ref)

    c = _dot_left_exact(tri_ref[...], lf) + carry_ref[0:1, :]
    c_ref[...] = c
    carry_ref[...] = jnp.broadcast_to(c[tm - 1:tm, :], carry_ref.shape)


def _inproj(x, g, w1, bfl, gains, rope, seg, segt, *, tm, tiles_per_seq, kv_transposed):
    m, d_model = x.shape
    n_w = w1.shape[1]
    tri = jnp.asarray(np.tril(np.ones((tm, tm), np.float32)), BF16)
    rope_tiles = rope.shape[0] // tm
    row = lambda n: pl.BlockSpec((tm, n), lambda t: (t, 0))
    tok = jax.ShapeDtypeStruct((m, W_ATT), F32)
    if kv_transposed:
        tps = tiles_per_seq
        kv = jax.ShapeDtypeStruct((m // (tps * tm), W_ATT, tps * tm), F32)
        kv_spec = pl.BlockSpec((1, W_ATT, tm), lambda t: (t // tps, 0, t % tps))
    else:
        kv, kv_spec = tok, row(W_ATT)
    outs = [tok, kv, kv, tok, kv, kv] + [jax.ShapeDtypeStruct((m, LANES), F32)] * 2 \
        + [jax.ShapeDtypeStruct((m, d_model), F32)] * 2
    qkv_specs = [row(W_ATT), kv_spec, kv_spec] * 2
    return pl.pallas_call(
        functools.partial(_inproj_body, tiles_per_seq=tiles_per_seq, kv_transposed=kv_transposed),
        grid=(m // tm,),
        in_specs=[row(d_model), _const_spec((1, d_model)), _const_spec((d_model, n_w)),
                  _const_spec((1, LANES)), _const_spec((SUBLANES, W_ATT)),
                  pl.BlockSpec((tm, 3 * LANES), lambda t: (t % rope_tiles, 0)),
                  _const_spec((W_ATT, LANES)), _const_spec((LANES, W_ATT)), _const_spec((tm, tm))],
        out_specs=qkv_specs + [row(LANES)] * 2 + [row(d_model)] * 2,
        out_shape=outs,
        scratch_shapes=[pltpu.VMEM((SUBLANES, LANES), F32)],
        compiler_params=_params(("arbitrary",)),
        name="inproj",
    )(x, g, w1, bfl, gains, rope, seg, segt, tri)


def _softmax_tile(s, carry, v_t):
    m, l, acc = carry
    m_new = jnp.maximum(m, jnp.max(s, axis=0, keepdims=True))
    alpha = jnp.exp(m - m_new)
    p = jnp.exp(s - m_new)
    l = alpha * l + jnp.sum(p, axis=0, keepdims=True)
    acc = alpha * acc + _dot(v_t, _bf(p))
    return m_new, l, acc


def _softmax_init():
    return (jnp.full((1, 2 * ATT_TILE), NEG, F32), jnp.zeros((1, 2 * ATT_TILE), F32),
            jnp.zeros((LANES, 2 * ATT_TILE), F32))


def _split3_f32(x):
    hi = _bf(x).astype(F32)
    r = x - hi
    mid = _bf(r).astype(F32)
    return hi, mid, _bf(r - mid).astype(F32)


def _head_rows(hh):
    row = lax.broadcasted_iota(jnp.int32, (LANES, 1), 0)
    return (row >= hh * HEAD_DIM) & (row < (hh + 1) * HEAD_DIM)


def _head_lanes(hh):
    lane = lax.broadcasted_iota(jnp.int32, (1, LANES), 1)
    return (lane >= hh * HEAD_DIM) & (lane < (hh + 1) * HEAD_DIM)


def _causal_mask():
    kpos = lax.broadcasted_iota(jnp.int32, (ATT_TILE, ATT_TILE), 0)
    qpos = lax.broadcasted_iota(jnp.int32, (ATT_TILE, ATT_TILE), 1)
    return kpos <= qpos


def _fox_bias_lanes(c, hp):
    lane = lax.broadcasted_iota(jnp.int32, (1, LANES), 1)
    aug = jnp.zeros(c.shape, F32)
    for hh in range(2):
        ck = jnp.sum(jnp.where(lane == 2 * hp + hh, c, 0.0), axis=1, keepdims=True)
        base = SUBLANES * hh
        for off, part in enumerate(_split3_f32(ck)):
            aug = jnp.where(lane == base + off, -part, aug)
        aug = jnp.where((lane >= base + 3) & (lane < base + 6), 1.0, aug)
    return aug


def _fox_bias_rows(cq, hh):
    row = lax.broadcasted_iota(jnp.int32, (LANES, 1), 0)
    base = SUBLANES * hh
    aug = jnp.where((row >= base) & (row < base + 3), 1.0, jnp.zeros((LANES, cq.shape[1]), F32))
    for off, part in enumerate(_split3_f32(cq)):
        aug = jnp.where(row == base + 3 + off, part, aug)
    return aug


def _moba_select(q_ref, km, sel_ref, ns, nb):
    blk = lax.broadcasted_iota(jnp.int32, (nb, 1), 0)
    qlane = lax.broadcasted_iota(jnp.int32, (1, ATT_TILE), 1)
    for j in range(ns):
        own = jnp.zeros((1, ATT_TILE), jnp.int32) + BLK_PER_TILE * j
        for e in range(1, BLK_PER_TILE):
            own = jnp.where(qlane >= e * MOBA_BLOCK, BLK_PER_TILE * j + e, own)
        past = blk < own
        qt_hi, qt_lo = _split2(q_ref[0, j * ATT_TILE:(j + 1) * ATT_TILE, :].T)
        for hh in range(2):
            km_hi, km_lo = _split2(jnp.where(_head_lanes(hh), km, 0.0))
            s = _dot(km_hi, qt_hi) + _dot(km_hi, qt_lo) + _dot(km_lo, qt_hi)
            sm = jnp.where(past, s, -jnp.inf)
            sel = jnp.zeros((nb, ATT_TILE), F32)
            for n in range(BLK_PER_TILE * (j + 1) - 1):
                sn = sm[n:n + 1, :]
                beats = (sm > sn) | ((sm == sn) & (blk < n))
                rank = jnp.sum(beats.astype(F32), axis=0, keepdims=True)
                ok = (rank < MOBA_TOPK) & (jnp.abs(sn) < jnp.inf)
                sel = jnp.where(blk == n, ok.astype(F32), sel)
            sel_ref[hh, j] = jnp.where(blk == own, 1.0, sel)


def _prompt_attn_body(pt_ref, *refs, ns, nb, fox, dec):
    n_in = 5 if fox else 3
    n_sc = 3 if fox else 4
    attn_in, refs = refs[:n_in], refs[n_in:]
    dec_pools, refs = refs[:len(dec.pools)], refs[len(dec.pools):]
    dec_in, refs = refs[:len(dec.in_specs)], refs[len(dec.in_specs):]
    (o_ref, dec_o_ref), refs = refs[:2], refs[2:]
    attn_sc, dec_sc = refs[:n_sc], refs[n_sc:]
    assert len(dec_sc) == len(dec.scratch)
    if fox:
        q_ref, k_ref, v_ref, ct_ref, c_ref = attn_in
        qtm_ref, vt_ref, kb_ref = attn_sc
    else:
        q_ref, k_ref, v_ref = attn_in
        qtm_ref, vt_ref, kb_ref, sel_ref = attn_sc
    hp = pl.program_id(1)
    i = pl.program_id(2)
    step = (pl.program_id(0) * pl.num_programs(1) + hp) * ns + i
    n_steps = pl.num_programs(0) * pl.num_programs(1) * ns
    dec_j = step % dec.steps_per_seq

    @pl.when(i == 0)
    def _():
        blk = lax.broadcasted_iota(jnp.int32, (nb, 1), 0)
        km = jnp.zeros((nb, LANES), F32)
        for j in range(ns):
            rows = slice(j * ATT_TILE, (j + 1) * ATT_TILE)
            kf = k_ref[0, :, rows].T
            kb_ref[rows, 0:LANES] = _bf(kf)
            vt_ref[j] = _bf(v_ref[0, :, rows])
            if not fox:
                for e in range(BLK_PER_TILE):
                    ksum = jnp.sum(kf[e * MOBA_BLOCK:(e + 1) * MOBA_BLOCK, :], axis=0, keepdims=True)
                    km = jnp.where(blk == BLK_PER_TILE * j + e, ksum * (1.0 / MOBA_BLOCK), km)
        if fox:
            kb_ref[:, LANES:2 * LANES] = _bf(_fox_bias_lanes(c_ref[0], hp))
        else:
            _moba_select(q_ref, km, sel_ref, ns, nb)
        for j in range(ns):
            rows = slice(j * ATT_TILE, (j + 1) * ATT_TILE)
            qt = (q_ref[0, rows, :] * SCALE).T
            for hh in range(2):
                cols = slice(hh * ATT_TILE, (hh + 1) * ATT_TILE)
                qtm_ref[j, 0:LANES, cols] = _bf(jnp.where(_head_rows(hh), qt, 0.0))
                if fox:
                    cq = ct_ref[0, pl.ds(2 * hp + hh, 1), rows]
                    qtm_ref[j, LANES:2 * LANES, cols] = _bf(_fox_bias_rows(cq, hh))

    causal = jnp.concatenate([_causal_mask()] * 2, axis=1)

    def tile(n, carry, diag, width=1):
        rows = pl.ds(pl.multiple_of(n * ATT_TILE, ATT_TILE), width * ATT_TILE)
        s = _dot(kb_ref[rows, :], qtm_ref[i])
        if fox:
            keep = causal if diag else None
        else:
            keep = jnp.concatenate(
                [jnp.concatenate([jnp.broadcast_to(sel_ref[hh, i, pl.ds(BLK_PER_TILE * n + e, 1), :],
                                                   (MOBA_BLOCK, ATT_TILE)) for hh in range(2)], axis=1)
                 for e in range(BLK_PER_TILE * width)], axis=0) > 0.5
            if diag:
                keep = keep & causal
        if keep is not None:
            s = jnp.where(keep, s, NEG)
        v_t = jnp.concatenate([vt_ref[n + w] for w in range(width)], axis=1) if width > 1 else vt_ref[n]
        return _softmax_tile(s, carry, v_t)

    dec.main(pt_ref, step, n_steps, dec_pools, dec_in, dec_sc)
    carry = tile(i, _softmax_init(), True)
    carry = lax.fori_loop(0, i // 2, lambda n2, c: tile(2 * n2, c, False, width=2), carry)
    _, l, acc = lax.cond(i % 2 == 1, lambda c: tile(i - 1, c, False), lambda c: c, carry)
    out = acc / l
    o_ref[0] = jnp.concatenate([out[hh * HEAD_DIM:(hh + 1) * HEAD_DIM, hh * ATT_TILE:(hh + 1) * ATT_TILE]
                                for hh in range(2)], axis=0).T

    @pl.when(dec_j == dec.steps_per_seq - 1)
    def _():
        dec.final(dec_in, dec_o_ref, dec_sc)


class _DecodeStream(NamedTuple):
    steps_per_seq: int
    pools: tuple
    in_specs: list
    args: tuple
    out_spec: pl.BlockSpec
    out_shape: jax.ShapeDtypeStruct
    scratch: list
    main: Callable
    final: Callable


def _page_copies(pt_ref, step, slot, pools, bufs, sems, *, spd, pg, last_first):
    n_pages = pt_ref.shape[1]
    seq, j = step // spd, step % spd
    copies = []
    for i in range(pg):
        col = n_pages - 1 - (j * pg + i) if last_first else j * pg + i
        page = pt_ref[seq, col]
        for pool, buf, sem in zip(pools, bufs, sems):
            copies.append(pltpu.make_async_copy(pool.at[page], buf.at[slot, i], sem.at[slot]))
    return copies


def _double_buffer(step, n_steps, copies):
    slot = step % 2

    @pl.when(step == 0)
    def _():
        for c in copies(step, slot):
            c.start()

    @pl.when(step + 1 < n_steps)
    def _():
        for c in copies(step + 1, 1 - slot):
            c.start()

    for c in copies(step, slot):
        c.wait()
    return slot


def _stream_pages(pt_ref, step, n_steps, pools, bufs, sems, **kw):
    return _double_buffer(step, n_steps, lambda s, half: _page_copies(pt_ref, s, half, pools, bufs, sems, **kw))


def _prompt_attention(page_table, dec, q, k, v, c=None, c_t=None):
    b, t_len, _ = q.shape
    ns = t_len // ATT_TILE
    nb = t_len // MOBA_BLOCK
    n_hp = W_ATT // LANES
    fox = c is not None
    kdim = 2 * LANES if fox else LANES
    spd = dec.steps_per_seq
    assert b * n_hp * ns == page_table.shape[0] * spd

    def dec_map(imap):
        def wrapped(bi, hp, i, pt):
            step = (bi * n_hp + hp) * ns + i
            return imap(step // spd, step % spd, pt)
        return wrapped

    dec_in_specs = [pl.BlockSpec(memory_space=pl.ANY)] * len(dec.pools)
    dec_in_specs += [pl.BlockSpec(s.block_shape, dec_map(s.index_map)) for s in dec.in_specs]
    dec_out_spec = pl.BlockSpec(dec.out_spec.block_shape, dec_map(dec.out_spec.index_map))
    q_spec = pl.BlockSpec((1, t_len, LANES), lambda bi, hp, i, pt: (bi, 0, hp))
    kv_spec = pl.BlockSpec((1, LANES, t_len), lambda bi, hp, i, pt: (bi, hp, 0))
    in_specs = [q_spec, kv_spec, kv_spec]
    scratch = [pltpu.VMEM((ns, kdim, 2 * ATT_TILE), BF16), pltpu.VMEM((ns, LANES, ATT_TILE), BF16),
               pltpu.VMEM((t_len, kdim), BF16)]
    if fox:
        in_specs += [pl.BlockSpec((1, N_HEADS, t_len), lambda bi, hp, i, pt: (bi, 0, 0)),
                     pl.BlockSpec((1, t_len, LANES), lambda bi, hp, i, pt: (bi, 0, 0))]
        args = (q, k, v, c_t, c)
    else:
        args = (q, k, v)
        scratch.append(pltpu.VMEM((2, ns, nb, ATT_TILE), F32))
    grid_spec = pltpu.PrefetchScalarGridSpec(
        num_scalar_prefetch=1,
        grid=(b, n_hp, ns),
        in_specs=in_specs + dec_in_specs,
        out_specs=[pl.BlockSpec((1, ATT_TILE, LANES), lambda bi, hp, i, pt: (bi, i, hp)), dec_out_spec],
        scratch_shapes=scratch + list(dec.scratch),
    )
    return pl.pallas_call(
        functools.partial(_prompt_attn_body, ns=ns, nb=nb, fox=fox, dec=dec),
        grid_spec=grid_spec,
        out_shape=[jax.ShapeDtypeStruct((b, t_len, W_ATT), F32), dec.out_shape],
        compiler_params=_params(("arbitrary", "arbitrary", "arbitrary")),
        name="fox_prompt_decode" if fox else "moba_prompt_scores",
    )(page_table, *args, *dec.pools, *dec.args)


def _ffn_body(*refs, decode, ck, rider_hps):
    if not decode:
        (pt_ref, top_ref), refs = refs[:2], refs[2:]
    x_ref, ya_ref, yb_ref, sga_ref, sgb_ref, p_ref = refs[:6]
    refs = refs[6:]
    if decode:
        buf0_ref, buf1_ref = refs[:2]
        refs = refs[2:]
    (wbm_ref, wbf_ref, wo_ref, g_ref, wup_ref, wgate_ref, cw_ref, cb_ref, wdown_ref,
     gp_ref, wpg_ref, wple_ref), refs = refs[:12], refs[12:]
    if decode:
        o_ref, tail_ref = refs
    else:
        pool_k_ref, pool_v_ref, rq_ref, rk_ref, rv_ref, o_ref, tail_ref, ro_ref = refs[:8]
        carry_ref, kbuf, vbuf, ksem, vsem = refs[8:]

        @pl.when(pl.program_id(1) == 0)
        def _():
            carry_ref[...] = jnp.zeros_like(carry_ref)

        step = pl.program_id(0) * pl.num_programs(1) + pl.program_id(1)
        _moba_decode_step(pt_ref, top_ref, step, pl.num_programs(0) * pl.num_programs(1),
                          (pool_k_ref, pool_v_ref), rq_ref, rk_ref, rv_ref, ro_ref,
                          (kbuf, vbuf), (ksem, vsem), hps=rider_hps)

    merged = (sga_ref[...] * _dot(_bf(ya_ref[...]), wbm_ref[...])
              + sgb_ref[...] * _dot(_bf(yb_ref[...]), wbf_ref[...]))
    x = x_ref[...] + _dot(_bf(merged), wo_ref[...])
    tm = x.shape[0]
    d_ff = wup_ref.shape[1]
    h = _bf(_rms(x) * g_ref[...])
    acc = jnp.zeros_like(x)
    for c in range(d_ff // ck):
        cols = slice(c * ck, (c + 1) * ck)
        u = _dot(h, wup_ref[:, cols])
        a = _dot(h, wgate_ref[:, cols])
        if decode:
            prev2 = buf0_ref[:, cols]
            prev1 = buf1_ref[:, cols]
            tail_ref[:, cols] = a
        else:
            ext = jnp.concatenate([carry_ref[:, cols], a], axis=0)
            prev1 = pltpu.roll(ext, 1, 0)[SUBLANES:, :]
            prev2 = pltpu.roll(ext, 2, 0)[SUBLANES:, :]
            carry_ref[:, cols] = a[tm - SUBLANES:, :]
            tail_ref[0, :, cols] = a[tm - SUBLANES:, :]
        conv = cb_ref[:, cols] + cw_ref[0:1, cols] * prev2 + cw_ref[1:2, cols] * prev1 + cw_ref[2:3, cols] * a
        act = conv * _sigmoid(conv) * u
        acc = acc + _dot(_bf(act), wdown_ref[cols, :])
    x = x + acc
    gate = _sigmoid(_dot(_bf(_rms(x) * gp_ref[...]), wpg_ref[...]))
    o_ref[...] = x + _dot(_bf(p_ref[...]), wple_ref[...]) * gate


def _ffn(x, ya, yb, sga, sgb, p, bufs, weights, *, n_seq, tm, ck, rider=None):
    m, d_model = x.shape
    d_ff = weights[4].shape[1]
    decode = bufs is not None
    w_specs = [_const_spec(w.shape) for w in weights]
    acts = (x, ya, yb, sga, sgb, p)
    if decode:
        row = lambda n: pl.BlockSpec((tm, n), lambda t: (t, 0))
        return pl.pallas_call(
            functools.partial(_ffn_body, decode=True, ck=ck, rider_hps=None),
            grid=(m // tm,),
            in_specs=[row(a.shape[1]) for a in acts] + [row(d_ff), row(d_ff)] + w_specs,
            out_specs=[row(d_model), row(d_ff)],
            out_shape=[jax.ShapeDtypeStruct((m, d_model), F32), jax.ShapeDtypeStruct((m, d_ff), F32)],
            compiler_params=_params(("arbitrary",)),
            name="ffn_decode",
        )(*acts, *bufs, *weights)
    tps = m // n_seq // tm
    hps = rider.hps
    gps = N_HEADS // hps
    db = rider.page_table.shape[0]
    assert n_seq * tps == db * gps
    row = lambda n: pl.BlockSpec((tm, n), lambda s, t, pt, top: (s * tps + t, 0))
    vec_spec = pl.BlockSpec((1, hps * HEAD_DIM, 1),
                            lambda s, t, pt, top: ((s * tps + t) // gps, (s * tps + t) % gps, 0))
    page_buf = pltpu.VMEM((2, hps * N_PG_MOBA, HEAD_DIM, PAGE_SIZE), F32)
    grid_spec = pltpu.PrefetchScalarGridSpec(
        num_scalar_prefetch=2,
        grid=(n_seq, tps),
        in_specs=[row(a.shape[1]) for a in acts] + w_specs + [pl.BlockSpec(memory_space=pl.ANY)] * 2
        + [vec_spec] * 3,
        out_specs=[row(d_model), pl.BlockSpec((1, SUBLANES, d_ff), lambda s, t, pt, top: (s, 0, 0)), vec_spec],
        scratch_shapes=[pltpu.VMEM((SUBLANES, d_ff), F32), page_buf, page_buf,
                        pltpu.SemaphoreType.DMA((2,)), pltpu.SemaphoreType.DMA((2,))],
    )
    return pl.pallas_call(
        functools.partial(_ffn_body, decode=False, ck=ck, rider_hps=hps),
        grid_spec=grid_spec,
        out_shape=[jax.ShapeDtypeStruct((m, d_model), F32), jax.ShapeDtypeStruct((n_seq, SUBLANES, d_ff), F32),
                   jax.ShapeDtypeStruct((db, W_ATT, 1), F32)],
        compiler_params=_params(("arbitrary", "arbitrary")),
        name="ffn_prompt_moba_decode",
    )(rider.page_table, rider.top_flat, *acts, *weights, rider.pool_k, rider.pool_v,
      rider.q_col, rider.k_col, rider.v_col)


def _moba_score_main(pt_ref, step, n_steps, pools, in_refs, sc_refs, *, spd, pg):
    q_ref, = in_refs
    sc_ref, kbuf, sem = sc_refs
    slot = _stream_pages(pt_ref, step, n_steps, pools, (kbuf,), (sem,), spd=spd, pg=pg, last_first=False)
    j = step % spd
    bps = pg // 2
    lane = lax.broadcasted_iota(jnp.int32, (1, LANES), 1)

    @pl.when(j == 0)
    def _():
        sc_ref[...] = jnp.full_like(sc_ref, -jnp.inf)

    qb = jnp.broadcast_to(q_ref[0], (W_ATT, PAGE_SIZE))
    sc = sc_ref[...]
    for m in range(bps):
        t = (kbuf[slot, 2 * m] + kbuf[slot, 2 * m + 1]) * qb
        per_pos = jnp.sum(t.reshape(N_HEADS, HEAD_DIM, PAGE_SIZE), axis=1)
        score = jnp.sum(per_pos, axis=1, keepdims=True) * (1.0 / MOBA_BLOCK)
        sc = jnp.where(lane == j * bps + m, score, sc)
    sc_ref[...] = sc


def _moba_score_final(in_refs, top_ref, sc_refs):
    del in_refs
    lane = lax.broadcasted_iota(jnp.int32, (1, LANES), 1)
    s = sc_refs[0][...]
    idx = lane.astype(F32)
    top = jnp.zeros((N_HEADS, LANES), F32)
    for r in range(MOBA_TOPK):
        best = jnp.max(s, axis=1, keepdims=True)
        arg = jnp.min(jnp.where(s == best, idx, float(LANES)), axis=1, keepdims=True)
        top = jnp.where(lane == r, arg, top)
        s = jnp.where(idx == arg, -jnp.inf, s)
    top_ref[0] = top.astype(jnp.int32)


def _moba_score_stream(page_table, pool_k, q_col, *, pg):
    db, n_pages = page_table.shape
    assert MOBA_TOPK <= n_pages // 2 <= LANES and pg % 2 == 0 and n_pages % pg == 0
    spd = n_pages // pg
    return _DecodeStream(
        steps_per_seq=spd,
        pools=(pool_k,),
        in_specs=[pl.BlockSpec((1, W_ATT, 1), lambda b, j, pt: (b, 0, 0))],
        args=(q_col,),
        out_spec=pl.BlockSpec((1, N_HEADS, LANES), lambda b, j, pt: (b, 0, 0)),
        out_shape=jax.ShapeDtypeStruct((db, N_HEADS, LANES), jnp.int32),
        scratch=[pltpu.VMEM((N_HEADS, LANES), F32), pltpu.VMEM((2, pg, W_ATT, PAGE_SIZE), F32),
                 pltpu.SemaphoreType.DMA((2,))],
        main=functools.partial(_moba_score_main, spd=spd, pg=pg),
        final=_moba_score_final,
    )


class _MobaDecodeRider(NamedTuple):
    page_table: jax.Array
    top_flat: jax.Array
    pool_k: jax.Array
    pool_v: jax.Array
    q_col: jax.Array
    k_col: jax.Array
    v_col: jax.Array
    hps: int


N_PG_MOBA = 2 * MOBA_TOPK


def _moba_decode_copies(pt_ref, top_ref, step, slot, pools, bufs, sems, *, hps):
    gps = N_HEADS // hps
    seq, g = step // gps, step % gps
    copies = []
    for hl in range(hps):
        h = g * hps + hl
        rows = pl.ds(pl.multiple_of(h * HEAD_DIM, HEAD_DIM), HEAD_DIM)
        for r in range(MOBA_TOPK):
            blk = top_ref[(seq * N_HEADS + h) * MOBA_TOPK + r]
            for e in range(2):
                page = pt_ref[seq, 2 * blk + e]
                i = (hl * MOBA_TOPK + r) * 2 + e
                for pool, buf, sem in zip(pools, bufs, sems):
                    copies.append(pltpu.make_async_copy(pool.at[page, rows], buf.at[slot, i], sem.at[slot]))
    return copies


def _moba_decode_step(pt_ref, top_ref, step, n_steps, pools, q_ref, kn_ref, vn_ref, o_ref, bufs, sems, *, hps):
    kbuf, vbuf = bufs
    slot = _double_buffer(step, n_steps, lambda s, half: _moba_decode_copies(
        pt_ref, top_ref, s, half, pools, bufs, sems, hps=hps))
    for hl in range(hps):
        dims = slice(hl * HEAD_DIM, (hl + 1) * HEAD_DIM)
        pages = range(hl * N_PG_MOBA, (hl + 1) * N_PG_MOBA)
        q = q_ref[0, dims, :] * SCALE
        logits = [jnp.sum(kbuf[slot, i] * q, axis=0, keepdims=True) for i in pages]
        l_new = jnp.sum(q * kn_ref[0, dims, :], axis=0, keepdims=True)
        s_max = logits[0]
        for s in logits[1:]:
            s_max = jnp.maximum(s_max, s)
        m = jnp.maximum(l_new, jnp.max(s_max, axis=1, keepdims=True))
        p_new = jnp.exp(l_new - m)
        p_sum = jnp.zeros((1, PAGE_SIZE), F32)
        pv = jnp.zeros((HEAD_DIM, PAGE_SIZE), F32)
        for i, s in zip(pages, logits):
            p = jnp.exp(s - m)
            p_sum = p_sum + p
            pv = pv + vbuf[slot, i] * p
        denom = p_new + jnp.sum(p_sum, axis=1, keepdims=True)
        acc = p_new * vn_ref[0, dims, :] + jnp.sum(pv, axis=1, keepdims=True)
        o_ref[0, dims, :] = acc / denom


def _fox_q8(q_ref):
    head = lax.broadcasted_iota(jnp.int32, (N_HEADS, 1), 0)
    lane = lax.broadcasted_iota(jnp.int32, (1, W_ATT), 1)
    own = (lane >= head * HEAD_DIM) & (lane < (head + 1) * HEAD_DIM)
    return jnp.where(own, q_ref[0] * SCALE, 0.0), own


def _fox_decode_main(pt_ref, step, n_steps, pools, in_refs, sc_refs, *, spd, pg):
    q_ref, _, _, cn_ref, u_ref = in_refs
    m_ref, l_ref, acc_ref, sfx_ref, kbuf, vbuf, lfbuf, ksem, vsem, lfsem = sc_refs
    slot = _stream_pages(pt_ref, step, n_steps, pools, (kbuf, vbuf, lfbuf), (ksem, vsem, lfsem),
                         spd=spd, pg=pg, last_first=True)

    @pl.when(step % spd == 0)
    def _():
        m_ref[...] = jnp.full_like(m_ref, NEG)
        l_ref[...] = jnp.zeros_like(l_ref)
        acc_ref[...] = jnp.zeros_like(acc_ref)
        sfx_ref[...] = jnp.zeros_like(sfx_ref)

    q8b = _bf(_fox_q8(q_ref)[0])
    cn = cn_ref[0]
    m = m_ref[:, 0:1]
    l = l_ref[:, 0:1]
    sfx = sfx_ref[...]
    lfts = [lfbuf[slot, i] for i in range(pg)]
    within = _dot_right_exact(jnp.concatenate(lfts, axis=0), u_ref[...])
    logits = []
    for i in range(pg):
        r = within[i * N_HEADS:(i + 1) * N_HEADS, :] + sfx
        logits.append(_dot(q8b, _bf(kbuf[slot, i])) + cn + r)
        sfx = sfx + jnp.sum(lfts[i], axis=1, keepdims=True)
    s_max = logits[0]
    for s in logits[1:]:
        s_max = jnp.maximum(s_max, s)
    m_new = jnp.maximum(m, jnp.max(s_max, axis=1, keepdims=True))
    alpha = jnp.exp(m - m_new)
    p_sum = jnp.zeros((N_HEADS, PAGE_SIZE), F32)
    pv = jnp.zeros((N_HEADS, W_ATT), F32)
    for i, s in enumerate(logits):
        p = jnp.exp(s - m_new)
        p_sum = p_sum + p
        pv = pv + _dot_nt(_bf(p), _bf(vbuf[slot, i]))
    l = alpha * l + jnp.sum(p_sum, axis=1, keepdims=True)
    acc = alpha * acc_ref[...] + pv
    m = m_new
    m_ref[...] = jnp.broadcast_to(m, m_ref.shape)
    l_ref[...] = jnp.broadcast_to(l, l_ref.shape)
    acc_ref[...] = acc
    sfx_ref[...] = sfx


def _fox_decode_final(in_refs, o_ref, sc_refs):
    q_ref, kn_ref, vn_ref, cn_ref, _ = in_refs[-5:]
    m_ref, l_ref, acc_ref = sc_refs[:3]
    q8, own = _fox_q8(q_ref)
    m = m_ref[:, 0:1]
    cn1 = cn_ref[0][:, 0:1]
    l_new = jnp.sum(q8 * kn_ref[0], axis=1, keepdims=True) + cn1 - cn1
    m_fin = jnp.maximum(m, l_new)
    a = jnp.exp(m - m_fin)
    p_new = jnp.exp(l_new - m_fin)
    o8 = (a * acc_ref[...] + p_new * vn_ref[0]) / (a * l_ref[:, 0:1] + p_new)
    o_ref[0] = jnp.sum(jnp.where(own, o8, 0.0), axis=0, keepdims=True)


def _fox_decode_stream(page_table, pool_k, pool_v, pool_lft, q, k_new, v_new, cn, *, pg):
    db, n_pages = page_table.shape
    assert n_pages % pg == 0

    spd = n_pages // pg
    vec_spec = pl.BlockSpec((1, 1, W_ATT), lambda b, j, pt: (b, 0, 0))
    u = jnp.asarray(np.tril(np.ones((PAGE_SIZE, PAGE_SIZE), np.float32), -1), BF16)
    page_buf = pltpu.VMEM((2, pg, W_ATT, PAGE_SIZE), F32)
    return _DecodeStream(
        steps_per_seq=spd,
        pools=(pool_k, pool_v, pool_lft),
        in_specs=[vec_spec, vec_spec, vec_spec,
                  pl.BlockSpec((1, N_HEADS, LANES), lambda b, j, pt: (b, 0, 0)),
                  pl.BlockSpec((PAGE_SIZE, PAGE_SIZE), lambda b, j, pt: (0, 0))],
        args=(q, k_new, v_new, cn, u),
        out_spec=vec_spec,
        out_shape=jax.ShapeDtypeStruct((db, 1, W_ATT), F32),
        scratch=[pltpu.VMEM((N_HEADS, LANES), F32), pltpu.VMEM((N_HEADS, LANES), F32),
                 pltpu.VMEM((N_HEADS, W_ATT), F32), pltpu.VMEM((N_HEADS, LANES), F32),
                 page_buf, page_buf, pltpu.VMEM((2, pg, N_HEADS, PAGE_SIZE), F32)]
        + [pltpu.SemaphoreType.DMA((2,))] * 3,
        main=functools.partial(_fox_decode_main, spd=spd, pg=pg),
        final=_fox_decode_final,
    )


def _rope_table(pos):
    half = ROPE_DIMS // 2
    inv = ROPE_THETA ** (-jnp.arange(0, ROPE_DIMS, 2, dtype=F32) / ROPE_DIMS)
    ang = pos.astype(F32)[:, None] * inv[None, :]
    d = np.arange(LANES) % HEAD_DIM
    sel = jnp.asarray(d % half)
    cos = jnp.where(jnp.asarray(d < ROPE_DIMS)[None, :], jnp.cos(ang)[:, sel], 1.0)
    sin = jnp.sin(ang)[:, sel]
    sin_up = jnp.where(jnp.asarray(d < half)[None, :], -sin, 0.0)
    sin_dn = jnp.where(jnp.asarray((d >= half) & (d < ROPE_DIMS))[None, :], sin, 0.0)
    return jnp.concatenate([cos, sin_up, sin_dn], axis=1)


def kernel(x_prompt, x_sample, cache_moba_k, cache_moba_v, cache_fox_k, cache_fox_v, cache_fox_logf,
           state_ffn_conv, page_table, p_prompt, p_sample, norm_attn_g, w_in, b_forget,
           qnorm_moba, knorm_moba, qnorm_fox, knorm_fox, w_branch_moba, w_branch_fox, w_out,
           norm_ffn_g, w_up, w_gate, conv_w, conv_b, w_down, norm_ple_g, w_ple, w_ple_gate):
    b, t_len, d_model = x_prompt.shape
    db, ts, _ = x_sample.shape
    depth = w_in.shape[0]
    n_pages = page_table.shape[1]
    n_pool = cache_moba_k.shape[1]
    assert depth == 1 and ts == 1 and t_len % ATT_TILE == 0 and n_pages % 2 == 0
    past_len = n_pages * PAGE_SIZE
    d_ff = w_up.shape[2]
    n_att = 6 * W_ATT

    w_in0 = w_in[0]
    w_fl = jnp.pad(w_in0[:, n_att:n_att + N_HEADS], ((0, 0), (0, LANES - N_HEADS)))
    w1 = _bf(jnp.concatenate([w_in0[:, :n_att], w_in0[:, n_att + N_HEADS:], w_fl], axis=1))
    bfl = jnp.pad(b_forget[0], (0, LANES - N_HEADS))[None, :]
    gains = jnp.stack([jnp.tile(g[0], N_HEADS) for g in (qnorm_moba, knorm_moba, qnorm_fox, knorm_fox)])
    gains = jnp.pad(gains, ((0, SUBLANES - 4), (0, 0)))
    seg_np = (np.arange(W_ATT)[:, None] // HEAD_DIM == np.arange(LANES)[None, :]).astype(np.float32)
    seg = jnp.asarray(seg_np, BF16)
    segt = jnp.asarray(seg_np.T, BF16)
    g_attn = norm_attn_g[0][None, :]
    ffn_w = (_bf(w_branch_moba[0]), _bf(w_branch_fox[0]), _bf(w_out[0]),
             norm_ffn_g[0][None, :], _bf(w_up[0]), _bf(w_gate[0]),
             jnp.pad(conv_w[0], ((0, SUBLANES - CONV_W), (0, 0))), conv_b[0][None, :], _bf(w_down[0]),
             norm_ple_g[0][None, :], _bf(w_ple_gate[0]), _bf(w_ple[0]))

    tm = 256
    xp = x_prompt.reshape(b * t_len, d_model)
    rope_p = _rope_table(jnp.arange(t_len, dtype=jnp.int32))
    qa, ka, va, qb, kb, vb, lf, c, sga, sgb = _inproj(
        xp, g_attn, w1, bfl, gains, rope_p, seg, segt, tm=tm, tiles_per_seq=t_len // tm, kv_transposed=True)
    xs = x_sample.reshape(db, d_model)
    rope_s = _rope_table(jnp.full((db,), past_len, jnp.int32))
    qa_s, ka_s, va_s, qb_s, kb_s, vb_s, lf_s, _, sga_s, sgb_s = _inproj(
        xs, g_attn, w1, bfl, gains, rope_s, seg, segt, tm=db, tiles_per_seq=1, kv_transposed=False)

    attn_steps = b * (W_ATT // LANES) * (t_len // ATT_TILE)
    assert attn_steps % db == 0 and n_pages % (attn_steps // db) == 0
    pg = n_pages // (attn_steps // db)
    v3 = lambda a: a.reshape(db, 1, W_ATT)
    col = lambda a: a.reshape(db, W_ATT, 1)
    pool = lambda cache: cache[0].transpose(0, 2, 3, 1).reshape(n_pool, W_ATT, PAGE_SIZE)
    pool_mk = pool(cache_moba_k)
    r3 = lambda a: a.reshape(b, t_len, a.shape[-1])
    ya, top = _prompt_attention(page_table, _moba_score_stream(page_table, pool_mk, col(qa_s), pg=pg),
                                r3(qa), ka, va)
    c3 = r3(c)
    c_t = c3[:, :, :N_HEADS].transpose(0, 2, 1)
    cn = jnp.broadcast_to(lf_s[:, :N_HEADS, None], (db, N_HEADS, LANES))
    fox_stream = _fox_decode_stream(page_table, pool(cache_fox_k), pool(cache_fox_v),
                                    cache_fox_logf[0].transpose(0, 2, 1), v3(qb_s), v3(kb_s), v3(vb_s), cn, pg=pg)
    yb, yb_s = _prompt_attention(page_table, fox_stream, r3(qb), kb, vb, c3, c_t)
    top_flat = top[:, :, :MOBA_TOPK].reshape(-1)

    mlp_steps = b * t_len // tm
    assert (N_HEADS * db) % mlp_steps == 0
    rider = _MobaDecodeRider(page_table, top_flat, pool_mk, pool(cache_moba_v), col(qa_s), col(ka_s), col(va_s),
                             hps=N_HEADS * db // mlp_steps)
    y_p, tail_p, ya_s = _ffn(xp, ya.reshape(b * t_len, W_ATT), yb.reshape(b * t_len, W_ATT), sga, sgb,
                             p_prompt[0].reshape(b * t_len, -1), None, ffn_w, n_seq=b, tm=tm, ck=3072, rider=rider)
    bufs = (state_ffn_conv[0, :, 0, :], state_ffn_conv[0, :, 1, :])
    y_s, a_s = _ffn(xs, ya_s.reshape(db, W_ATT), yb_s.reshape(db, W_ATT), sga_s, sgb_s,
                    p_sample[0].reshape(db, -1), bufs, ffn_w, n_seq=db, tm=db, ck=512)

    heads_t = lambda a: a.reshape(1, b, N_HEADS, HEAD_DIM, t_len).transpose(0, 1, 4, 2, 3)
    heads = lambda a: a.reshape(1, db, ts, N_HEADS, HEAD_DIM)
    outs_p = (heads_t(ka), heads_t(va), heads_t(kb), heads_t(vb),
              lf[:, :N_HEADS].reshape(1, b, t_len, N_HEADS), tail_p[None, :, SUBLANES - (CONV_W - 1):, :])
    conv_s = jnp.stack([state_ffn_conv[0, :, 1, :], a_s], axis=1)[None]
    outs_s = (heads(ka_s), heads(va_s), heads(kb_s), heads(vb_s),
              lf_s[:, :N_HEADS].reshape(1, db, ts, N_HEADS), conv_s)
    return (y_p.reshape(b, t_len, d_model), y_s.reshape(db, ts, d_model)) + outs_p + outs_s
```

```python
import functools
from typing import Callable, NamedTuple

import jax
import jax.numpy as jnp
import numpy as np
from jax import lax
from jax.experimental import pallas as pl
from jax.experimental.pallas import tpu as pltpu

F32 = jnp.float32
BF16 = jnp.bfloat16

HEAD_DIM = 64
N_HEADS = 8
W_ATT = N_HEADS * HEAD_DIM
MOBA_BLOCK = 256
MOBA_TOPK = 3
PAGE_SIZE = 128
ROPE_THETA = 500000.0
ROPE_DIMS = HEAD_DIM // 4
CONV_W = 3
EPS = 1e-6
SCALE = HEAD_DIM ** -0.5
NEG = -1e30

LANES = 128
SUBLANES = 8
BLK_PER_TILE = 2
ATT_TILE = BLK_PER_TILE * MOBA_BLOCK
VMEM_LIMIT = 56 * 1024 * 1024


def _bf(x):
    return x.astype(BF16)


def _dot(a, b):
    return jnp.dot(a, b, preferred_element_type=F32)


def _dot_nt(a, b):
    return lax.dot_general(a, b, (((1,), (1,)), ((), ())), preferred_element_type=F32)


def _split2(x):
    hi = _bf(x)
    lo = _bf(x - hi.astype(F32))
    return hi, lo


def _split3(x):
    hi = _bf(x)
    r = x - hi.astype(F32)
    mid = _bf(r)
    lo = _bf(r - mid.astype(F32))
    return hi, mid, lo


def _dot_left_exact(a_bf, x):
    hi, mid, lo = _split3(x)
    return _dot(a_bf, hi) + _dot(a_bf, mid) + _dot(a_bf, lo)


def _dot_right_exact(x, b_bf):
    hi, mid, lo = _split3(x)
    return _dot(hi, b_bf) + _dot(mid, b_bf) + _dot(lo, b_bf)


def _sigmoid(x):
    return 1.0 / (1.0 + jnp.exp(-x))


def _rms(x):
    return x * lax.rsqrt(jnp.mean(x * x, axis=-1, keepdims=True) + EPS)


def _const_spec(shape):
    nd = len(shape)
    return pl.BlockSpec(shape, lambda *_: (0,) * nd, pipeline_mode=pl.Buffered(1))


def _params(sem):
    return pltpu.CompilerParams(dimension_semantics=sem, vmem_limit_bytes=VMEM_LIMIT)


def _inproj_body(x_ref, g_ref, w_ref, bfl_ref, gains_ref, rope_ref, seg_ref, segt_ref, tri_ref,
                 qa_ref, ka_ref, va_ref, qb_ref, kb_ref, vb_ref, lf_ref, c_ref, sga_ref, sgb_ref,
                 carry_ref, *, tiles_per_seq, kv_transposed):
    t = pl.program_id(0)
    tm = x_ref.shape[0]
    d_model = x_ref.shape[1]
    h = _bf(_rms(x_ref[...]) * g_ref[...])
    seg = seg_ref[...]
    segt = segt_ref[...]
    rope = rope_ref[...]

    def widen(tbl):
        return jnp.concatenate([tbl] * (W_ATT // LANES), axis=1)

    cos_t = widen(rope[:, 0:LANES])
    sin_up = widen(rope[:, LANES:2 * LANES])
    sin_dn = widen(rope[:, 2 * LANES:3 * LANES])

    att = _dot(h, w_ref[:, 0:6 * W_ATT])

    def proj(c0, n):
        if c0 + n <= 6 * W_ATT:
            return att[:, c0:c0 + n]
        return _dot(h, w_ref[:, c0:c0 + n])

    def headnorm(y, gain):
        ss = _dot(_bf(y * y), seg)
        r = lax.rsqrt(ss * (1.0 / HEAD_DIM) + EPS)
        r_hi, r_lo = _split2(r)
        return y * _dot(jnp.concatenate([r_hi, r_lo], axis=1), segt) * gain

    def rope_fn(y):
        half = ROPE_DIMS // 2
        return (y * cos_t + pltpu.roll(y, W_ATT - half, 1) * sin_up
                + pltpu.roll(y, half, 1) * sin_dn)

    def put_kv(ref, y):
        if kv_transposed:
            ref[0] = y.T
        else:
            ref[...] = y

    qa_ref[...] = rope_fn(headnorm(proj(0, W_ATT), gains_ref[0:1, :]))
    put_kv(ka_ref, rope_fn(headnorm(proj(W_ATT, W_ATT), gains_ref[1:2, :])))
    put_kv(va_ref, proj(2 * W_ATT, W_ATT))
    qb_ref[...] = headnorm(proj(3 * W_ATT, W_ATT), gains_ref[2:3, :])
    put_kv(kb_ref, headnorm(proj(4 * W_ATT, W_ATT), gains_ref[3:4, :]))
    put_kv(vb_ref, proj(5 * W_ATT, W_ATT))
    sga_ref[...] = _sigmoid(proj(6 * W_ATT, d_model))
    sgb_ref[...] = _sigmoid(proj(6 * W_ATT + d_model, d_model))

    fl = proj(6 * W_ATT + 2 * d_model, LANES) + bfl_ref[...]
    lf = jnp.minimum(fl, 0.0) - jnp.log1p(jnp.exp(-jnp.abs(fl)))
    lf_ref[...] = lf

    @pl.when(t % tiles_per_seq == 0)
    def _():
        carry_ref[...] = jnp.zeros_like(carry_ref)

    c = _dot_left_exact(tri_ref[...], lf) + carry_ref[0:1, :]
    c_ref[...] = c
    carry_ref[...] = jnp.broadcast_to(c[tm - 1:tm, :], carry_ref.shape)


def _inproj(x, g, w1, bfl, gains, rope, seg, segt, *, tm, tiles_per_seq, kv_transposed):
    m, d_model = x.shape
    n_w = w1.shape[1]
    tri = jnp.asarray(np.tril(np.ones((tm, tm), np.float32)), BF16)
    rope_tiles = rope.shape[0] // tm
    row = lambda n: pl.BlockSpec((tm, n), lambda t: (t, 0))
    tok = jax.ShapeDtypeStruct((m, W_ATT), F32)
    if kv_transposed:
        tps = tiles_per_seq
        kv = jax.ShapeDtypeStruct((m // (tps * tm), W_ATT, tps * tm), F32)
        kv_spec = pl.BlockSpec((1, W_ATT, tm), lambda t: (t // tps, 0, t % tps))
    else:
        kv, kv_spec = tok, row(W_ATT)
    outs = [tok, kv, kv, tok, kv, kv] + [jax.ShapeDtypeStruct((m, LANES), F32)] * 2 \
        + [jax.ShapeDtypeStruct((m, d_model), F32)] * 2
    qkv_specs = [row(W_ATT), kv_spec, kv_spec] * 2
    return pl.pallas_call(
        functools.partial(_inproj_body, tiles_per_seq=tiles_per_seq, kv_transposed=kv_transposed),
        grid=(m // tm,),
        in_specs=[row(d_model), _const_spec((1, d_model)), _const_spec((d_model, n_w)),
                  _const_spec((1, LANES)), _const_spec((SUBLANES, W_ATT)),
                  pl.BlockSpec((tm, 3 * LANES), lambda t: (t % rope_tiles, 0)),
                  _const_spec((W_ATT, LANES)), _const_spec((2 * LANES, W_ATT)), _const_spec((tm, tm))],
        out_specs=qkv_specs + [row(LANES)] * 2 + [row(d_model)] * 2,
        out_shape=outs,
        scratch_shapes=[pltpu.VMEM((SUBLANES, LANES), F32)],
        compiler_params=_params(("arbitrary",)),
        name="inproj",
    )(x, g, w1, bfl, gains, rope, seg, segt, tri)


def _softmax_tile(s, carry, v_t):
    m, l, acc = carry
    m_new = jnp.maximum(m, jnp.max(s, axis=0, keepdims=True))
    alpha = jnp.exp(m - m_new)
    p = jnp.exp(s - m_new)
    l = alpha * l + jnp.sum(p, axis=0, keepdims=True)
    acc = alpha * acc + _dot(v_t, _bf(p))
    return m_new, l, acc


def _softmax_init():
    return (jnp.full((1, 2 * ATT_TILE), NEG, F32), jnp.zeros((1, 2 * ATT_TILE), F32),
            jnp.zeros((LANES, 2 * ATT_TILE), F32))


def _split3_f32(x):
    hi = _bf(x).astype(F32)
    r = x - hi
    mid = _bf(r).astype(F32)
    return hi, mid, _bf(r - mid).astype(F32)


def _head_rows(hh):
    row = lax.broadcasted_iota(jnp.int32, (LANES, 1), 0)
    return (row >= hh * HEAD_DIM) & (row < (hh + 1) * HEAD_DIM)


def _head_lanes(hh):
    lane = lax.broadcasted_iota(jnp.int32, (1, LANES), 1)
    return (lane >= hh * HEAD_DIM) & (lane < (hh + 1) * HEAD_DIM)


def _causal_mask():
    kpos = lax.broadcasted_iota(jnp.int32, (ATT_TILE, ATT_TILE), 0)
    qpos = lax.broadcasted_iota(jnp.int32, (ATT_TILE, ATT_TILE), 1)
    return kpos <= qpos


def _fox_bias_lanes(c, hp):
    lane = lax.broadcasted_iota(jnp.int32, (1, LANES), 1)
    aug = jnp.zeros(c.shape, F32)
    for hh in range(2):
        ck = jnp.sum(jnp.where(lane == 2 * hp + hh, c, 0.0), axis=1, keepdims=True)
        base = SUBLANES * hh
        for off, part in enumerate(_split3_f32(ck)):
            aug = jnp.where(lane == base + off, -part, aug)
        aug = jnp.where((lane >= base + 3) & (lane < base + 6), 1.0, aug)
    return aug


def _fox_bias_rows(cq, hh):
    row = lax.broadcasted_iota(jnp.int32, (LANES, 1), 0)
    base = SUBLANES * hh
    aug = jnp.where((row >= base) & (row < base + 3), 1.0, jnp.zeros((LANES, cq.shape[1]), F32))
    for off, part in enumerate(_split3_f32(cq)):
        aug = jnp.where(row == base + 3 + off, part, aug)
    return aug


def _moba_select(q_ref, km, sel_ref, ns, nb):
    blk = lax.broadcasted_iota(jnp.int32, (nb, 1), 0)
    qlane = lax.broadcasted_iota(jnp.int32, (1, ATT_TILE), 1)
    for j in range(ns):
        own = jnp.zeros((1, ATT_TILE), jnp.int32) + BLK_PER_TILE * j
        for e in range(1, BLK_PER_TILE):
            own = jnp.where(qlane >= e * MOBA_BLOCK, BLK_PER_TILE * j + e, own)
        past = blk < own
        qt_hi, qt_lo = _split2(q_ref[0, j * ATT_TILE:(j + 1) * ATT_TILE, :].T)
        for hh in range(2):
            km_hi, km_lo = _split2(jnp.where(_head_lanes(hh), km, 0.0))
            s = _dot(km_hi, qt_hi) + _dot(km_hi, qt_lo) + _dot(km_lo, qt_hi)
            sm = jnp.where(past, s, -jnp.inf)
            sel = jnp.zeros((nb, ATT_TILE), F32)
            for n in range(BLK_PER_TILE * (j + 1) - 1):
                sn = sm[n:n + 1, :]
                beats = (sm > sn) | ((sm == sn) & (blk < n))
                rank = jnp.sum(beats.astype(F32), axis=0, keepdims=True)
                ok = (rank < MOBA_TOPK) & (jnp.abs(sn) < jnp.inf)
                sel = jnp.where(blk == n, ok.astype(F32), sel)
            sel_ref[hh, j] = jnp.where(blk == own, 1.0, sel)


def _prompt_attn_body(pt_ref, *refs, ns, nb, fox, dec):
    n_in = 5 if fox else 3
    n_sc = 3 if fox else 4
    attn_in, refs = refs[:n_in], refs[n_in:]
    dec_pools, refs = refs[:len(dec.pools)], refs[len(dec.pools):]
    dec_in, refs = refs[:len(dec.in_specs)], refs[len(dec.in_specs):]
    (o_ref, dec_o_ref), refs = refs[:2], refs[2:]
    attn_sc, dec_sc = refs[:n_sc], refs[n_sc:]
    assert len(dec_sc) == len(dec.scratch)
    if fox:
        q_ref, k_ref, v_ref, ct_ref, c_ref = attn_in
        qtm_ref, vt_ref, kb_ref = attn_sc
    else:
        q_ref, k_ref, v_ref = attn_in
        qtm_ref, vt_ref, kb_ref, sel_ref = attn_sc
    hp = pl.program_id(1)
    i = pl.program_id(2)
    step = (pl.program_id(0) * pl.num_programs(1) + hp) * ns + i
    n_steps = pl.num_programs(0) * pl.num_programs(1) * ns
    dec_j = step % dec.steps_per_seq

    @pl.when(i == 0)
    def _():
        blk = lax.broadcasted_iota(jnp.int32, (nb, 1), 0)
        km = jnp.zeros((nb, LANES), F32)
        for j in range(ns):
            rows = slice(j * ATT_TILE, (j + 1) * ATT_TILE)
            kf = k_ref[0, :, rows].T
            kb_ref[rows, 0:LANES] = _bf(kf)
            vt_ref[j] = _bf(v_ref[0, :, rows])
            if not fox:
                for e in range(BLK_PER_TILE):
                    ksum = jnp.sum(kf[e * MOBA_BLOCK:(e + 1) * MOBA_BLOCK, :], axis=0, keepdims=True)
                    km = jnp.where(blk == BLK_PER_TILE * j + e, ksum * (1.0 / MOBA_BLOCK), km)
        if fox:
            kb_ref[:, LANES:2 * LANES] = _bf(_fox_bias_lanes(c_ref[0], hp))
        else:
            _moba_select(q_ref, km, sel_ref, ns, nb)
        for j in range(ns):
            rows = slice(j * ATT_TILE, (j + 1) * ATT_TILE)
            qt = (q_ref[0, rows, :] * SCALE).T
            for hh in range(2):
                cols = slice(hh * ATT_TILE, (hh + 1) * ATT_TILE)
                qtm_ref[j, 0:LANES, cols] = _bf(jnp.where(_head_rows(hh), qt, 0.0))
                if fox:
                    cq = ct_ref[0, pl.ds(2 * hp + hh, 1), rows]
                    qtm_ref[j, LANES:2 * LANES, cols] = _bf(_fox_bias_rows(cq, hh))

    causal = jnp.concatenate([_causal_mask()] * 2, axis=1)

    def tile(n, carry, diag, width=1):
        rows = pl.ds(pl.multiple_of(n * ATT_TILE, ATT_TILE), width * ATT_TILE)
        s = _dot(kb_ref[rows, :], qtm_ref[i])
        if fox:
            keep = causal if diag else None
        else:
            keep = jnp.concatenate(
                [jnp.concatenate([jnp.broadcast_to(sel_ref[hh, i, pl.ds(BLK_PER_TILE * n + e, 1), :],
                                                   (MOBA_BLOCK, ATT_TILE)) for hh in range(2)], axis=1)
                 for e in range(BLK_PER_TILE * width)], axis=0) > 0.5
            if diag:
                keep = keep & causal
        if keep is not None:
            s = jnp.where(keep, s, NEG)
        v_t = jnp.concatenate([vt_ref[n + w] for w in range(width)], axis=1) if width > 1 else vt_ref[n]
        return _softmax_tile(s, carry, v_t)

    dec.main(pt_ref, step, n_steps, dec_pools, dec_in, dec_sc)
    carry = tile(i, _softmax_init(), True)
    carry = lax.fori_loop(0, i // 2, lambda n2, c: tile(2 * n2, c, False, width=2), carry)
    _, l, acc = lax.cond(i % 2 == 1, lambda c: tile(i - 1, c, False), lambda c: c, carry)
    out = acc / l
    o_ref[0] = jnp.concatenate([out[hh * HEAD_DIM:(hh + 1) * HEAD_DIM, hh * ATT_TILE:(hh + 1) * ATT_TILE]
                                for hh in range(2)], axis=0).T

    @pl.when(dec_j == dec.steps_per_seq - 1)
    def _():
        dec.final(step // dec.steps_per_seq, dec_in, dec_o_ref, dec_sc)


class _DecodeStream(NamedTuple):
    steps_per_seq: int
    pools: tuple
    in_specs: list
    args: tuple
    out_spec: pl.BlockSpec
    out_shape: jax.ShapeDtypeStruct
    scratch: list
    main: Callable
    final: Callable


def _page_copies(pt_ref, step, slot, pools, bufs, sems, *, spd, pg, last_first):
    n_pages = pt_ref.shape[1]
    seq, j = step // spd, step % spd
    copies = []
    for i in range(pg):
        col = n_pages - 1 - (j * pg + i) if last_first else j * pg + i
        page = pt_ref[seq, col]
        for pool, buf, sem in zip(pools, bufs, sems):
            copies.append(pltpu.make_async_copy(pool.at[page], buf.at[slot, i], sem.at[slot]))
    return copies


def _double_buffer(step, n_steps, copies):
    slot = step % 2

    @pl.when(step == 0)
    def _():
        for c in copies(step, slot):
            c.start()

    @pl.when(step + 1 < n_steps)
    def _():
        for c in copies(step + 1, 1 - slot):
            c.start()

    for c in copies(step, slot):
        c.wait()
    return slot


def _stream_pages(pt_ref, step, n_steps, pools, bufs, sems, **kw):
    return _double_buffer(step, n_steps, lambda s, half: _page_copies(pt_ref, s, half, pools, bufs, sems, **kw))


def _prompt_attention(page_table, dec, q, k, v, c=None, c_t=None):
    b, t_len, _ = q.shape
    ns = t_len // ATT_TILE
    nb = t_len // MOBA_BLOCK
    n_hp = W_ATT // LANES
    fox = c is not None
    kdim = 2 * LANES if fox else LANES
    spd = dec.steps_per_seq
    assert b * n_hp * ns == page_table.shape[0] * spd

    def dec_map(imap):
        def wrapped(bi, hp, i, pt):
            step = (bi * n_hp + hp) * ns + i
            return imap(step // spd, step % spd, pt)
        return wrapped

    dec_in_specs = [pl.BlockSpec(memory_space=pl.ANY)] * len(dec.pools)
    dec_in_specs += [pl.BlockSpec(s.block_shape, dec_map(s.index_map)) for s in dec.in_specs]
    dec_out_spec = pl.BlockSpec(dec.out_spec.block_shape, dec_map(dec.out_spec.index_map))
    q_spec = pl.BlockSpec((1, t_len, LANES), lambda bi, hp, i, pt: (bi, 0, hp))
    kv_spec = pl.BlockSpec((1, LANES, t_len), lambda bi, hp, i, pt: (bi, hp, 0))
    in_specs = [q_spec, kv_spec, kv_spec]
    scratch = [pltpu.VMEM((ns, kdim, 2 * ATT_TILE), BF16), pltpu.VMEM((ns, LANES, ATT_TILE), BF16),
               pltpu.VMEM((t_len, kdim), BF16)]
    if fox:
        in_specs += [pl.BlockSpec((1, N_HEADS, t_len), lambda bi, hp, i, pt: (bi, 0, 0)),
                     pl.BlockSpec((1, t_len, LANES), lambda bi, hp, i, pt: (bi, 0, 0))]
        args = (q, k, v, c_t, c)
    else:
        args = (q, k, v)
        scratch.append(pltpu.VMEM((2, ns, nb, ATT_TILE), F32))
    grid_spec = pltpu.PrefetchScalarGridSpec(
        num_scalar_prefetch=1,
        grid=(b, n_hp, ns),
        in_specs=in_specs + dec_in_specs,
        out_specs=[pl.BlockSpec((1, ATT_TILE, LANES), lambda bi, hp, i, pt: (bi, i, hp)), dec_out_spec],
        scratch_shapes=scratch + list(dec.scratch),
    )
    return pl.pallas_call(
        functools.partial(_prompt_attn_body, ns=ns, nb=nb, fox=fox, dec=dec),
        grid_spec=grid_spec,
        out_shape=[jax.ShapeDtypeStruct((b, t_len, W_ATT), F32), dec.out_shape],
        compiler_params=_params(("arbitrary", "arbitrary", "arbitrary")),
        name="fox_prompt_decode" if fox else "moba_prompt_scores",
    )(page_table, *args, *dec.pools, *dec.args)


def _ffn_body(*refs, decode, rider_hps):
    if not decode:
        (pt_ref, top_ref), refs = refs[:2], refs[2:]
    x_ref, ya_ref, yb_ref, sga_ref, sgb_ref, p_ref = refs[:6]
    refs = refs[6:]
    if decode:
        buf0_ref, buf1_ref = refs[:2]
        refs = refs[2:]
    (wbm_ref, wbf_ref, wo_ref, g_ref, wug_ref, cw_ref, cb_ref, wdown_ref,
     gp_ref, wpg_ref, wple_ref), refs = refs[:11], refs[11:]
    if decode:
        o_ref, tail_ref = refs
    else:
        pool_k_ref, pool_v_ref, rq_ref, rk_ref, rv_ref, o_ref, tail_ref, ro_ref = refs[:8]
        carry_ref, kbuf, vbuf, ksem, vsem = refs[8:]

        @pl.when(pl.program_id(1) == 0)
        def _():
            carry_ref[...] = jnp.zeros_like(carry_ref)

        step = pl.program_id(0) * pl.num_programs(1) + pl.program_id(1)
        _moba_decode_step(pt_ref, top_ref, step, pl.num_programs(0) * pl.num_programs(1),
                          (pool_k_ref, pool_v_ref), rq_ref, rk_ref, rv_ref, ro_ref,
                          (kbuf, vbuf), (ksem, vsem), hps=rider_hps)

    merged = (sga_ref[...] * _dot(_bf(ya_ref[...]), wbm_ref[...])
              + sgb_ref[...] * _dot(_bf(yb_ref[...]), wbf_ref[...]))
    x = x_ref[...] + _dot(_bf(merged), wo_ref[...])
    tm = x.shape[0]
    d_ff = wdown_ref.shape[0]
    h = _bf(_rms(x) * g_ref[...])
    ua = _dot(h, wug_ref[...])
    u, a = ua[:, :d_ff], ua[:, d_ff:]
    if decode:
        prev2 = buf0_ref[...]
        prev1 = buf1_ref[...]
        tail_ref[...] = a
    else:
        ext = jnp.concatenate([carry_ref[...], a], axis=0)
        prev1 = pltpu.roll(ext, 1, 0)[SUBLANES:, :]
        prev2 = pltpu.roll(ext, 2, 0)[SUBLANES:, :]
        carry_ref[...] = a[tm - SUBLANES:, :]
        tail_ref[0] = a[tm - SUBLANES:, :]
    conv = cb_ref[...] + cw_ref[0:1, :] * prev2 + cw_ref[1:2, :] * prev1 + cw_ref[2:3, :] * a
    act = conv * _sigmoid(conv) * u
    x = x + _dot(_bf(act), wdown_ref[...])
    gate = _sigmoid(_dot(_bf(_rms(x) * gp_ref[...]), wpg_ref[...]))
    o_ref[...] = x + _dot(_bf(p_ref[...]), wple_ref[...]) * gate


def _ffn(x, ya, yb, sga, sgb, p, bufs, weights, *, n_seq, tm, rider=None):
    m, d_model = x.shape
    d_ff = weights[7].shape[0]
    decode = bufs is not None
    w_specs = [_const_spec(w.shape) for w in weights]
    acts = (x, ya, yb, sga, sgb, p)
    if decode:
        row = lambda n: pl.BlockSpec((tm, n), lambda t: (t, 0))
        return pl.pallas_call(
            functools.partial(_ffn_body, decode=True, rider_hps=None),
            grid=(m // tm,),
            in_specs=[row(a.shape[1]) for a in acts] + [row(d_ff), row(d_ff)] + w_specs,
            out_specs=[row(d_model), row(d_ff)],
            out_shape=[jax.ShapeDtypeStruct((m, d_model), F32), jax.ShapeDtypeStruct((m, d_ff), F32)],
            compiler_params=_params(("arbitrary",)),
            name="ffn_decode",
        )(*acts, *bufs, *weights)
    tps = m // n_seq // tm
    hps = rider.hps
    gps = N_HEADS // hps
    db = rider.page_table.shape[0]
    assert n_seq * tps == db * gps
    row = lambda n: pl.BlockSpec((tm, n), lambda s, t, pt, top: (s * tps + t, 0))
    vec_spec = pl.BlockSpec((1, hps * HEAD_DIM, 1),
                            lambda s, t, pt, top: ((s * tps + t) // gps, (s * tps + t) % gps, 0))
    page_buf = pltpu.VMEM((2, hps * N_PG_MOBA, HEAD_DIM, PAGE_SIZE), F32)
    grid_spec = pltpu.PrefetchScalarGridSpec(
        num_scalar_prefetch=2,
        grid=(n_seq, tps),
        in_specs=[row(a.shape[1]) for a in acts] + w_specs + [pl.BlockSpec(memory_space=pl.ANY)] * 2
        + [vec_spec] * 3,
        out_specs=[row(d_model), pl.BlockSpec((1, SUBLANES, d_ff), lambda s, t, pt, top: (s, 0, 0)), vec_spec],
        scratch_shapes=[pltpu.VMEM((SUBLANES, d_ff), F32), page_buf, page_buf,
                        pltpu.SemaphoreType.DMA((2,)), pltpu.SemaphoreType.DMA((2,))],
    )
    return pl.pallas_call(
        functools.partial(_ffn_body, decode=False, rider_hps=hps),
        grid_spec=grid_spec,
        out_shape=[jax.ShapeDtypeStruct((m, d_model), F32), jax.ShapeDtypeStruct((n_seq, SUBLANES, d_ff), F32),
                   jax.ShapeDtypeStruct((db, W_ATT, 1), F32)],
        compiler_params=_params(("arbitrary", "arbitrary")),
        name="ffn_prompt_moba_decode",
    )(rider.page_table, rider.top_flat, *acts, *weights, rider.pool_k, rider.pool_v,
      rider.q_col, rider.k_col, rider.v_col)


def _moba_score_main(pt_ref, step, n_steps, pools, in_refs, sc_refs, *, spd, pg):
    q_ref, = in_refs
    sc_ref, kbuf, sem = sc_refs
    slot = _stream_pages(pt_ref, step, n_steps, pools, (kbuf,), (sem,), spd=spd, pg=pg, last_first=False)
    j = step % spd
    bps = pg // 2
    lane = lax.broadcasted_iota(jnp.int32, (1, LANES), 1)

    @pl.when(j == 0)
    def _():
        sc_ref[...] = jnp.full_like(sc_ref, -jnp.inf)

    qb = jnp.broadcast_to(q_ref[0], (W_ATT, PAGE_SIZE))
    sc = sc_ref[...]
    for m in range(bps):
        t = (kbuf[slot, 2 * m] + kbuf[slot, 2 * m + 1]) * qb
        per_pos = jnp.sum(t.reshape(N_HEADS, HEAD_DIM, PAGE_SIZE), axis=1)
        score = jnp.sum(per_pos, axis=1, keepdims=True) * (1.0 / MOBA_BLOCK)
        sc = jnp.where(lane == j * bps + m, score, sc)
    sc_ref[...] = sc


def _moba_score_final(seq, in_refs, top_ref, sc_refs):
    del seq, in_refs
    lane = lax.broadcasted_iota(jnp.int32, (1, LANES), 1)
    s = sc_refs[0][...]
    idx = lane.astype(F32)
    top = jnp.zeros((N_HEADS, LANES), F32)
    for r in range(MOBA_TOPK):
        best = jnp.max(s, axis=1, keepdims=True)
        arg = jnp.min(jnp.where(s == best, idx, float(LANES)), axis=1, keepdims=True)
        top = jnp.where(lane == r, arg, top)
        s = jnp.where(idx == arg, -jnp.inf, s)
    top_ref[0] = top.astype(jnp.int32)


def _moba_score_stream(page_table, pool_k, q_col, *, pg):
    db, n_pages = page_table.shape
    assert MOBA_TOPK <= n_pages // 2 <= LANES and pg % 2 == 0 and n_pages % pg == 0
    spd = n_pages // pg
    return _DecodeStream(
        steps_per_seq=spd,
        pools=(pool_k,),
        in_specs=[pl.BlockSpec((1, W_ATT, 1), lambda b, j, pt: (b, 0, 0))],
        args=(q_col,),
        out_spec=pl.BlockSpec((1, N_HEADS, LANES), lambda b, j, pt: (b, 0, 0)),
        out_shape=jax.ShapeDtypeStruct((db, N_HEADS, LANES), jnp.int32),
        scratch=[pltpu.VMEM((N_HEADS, LANES), F32), pltpu.VMEM((2, pg, W_ATT, PAGE_SIZE), F32),
                 pltpu.SemaphoreType.DMA((2,))],
        main=functools.partial(_moba_score_main, spd=spd, pg=pg),
        final=_moba_score_final,
    )


class _MobaDecodeRider(NamedTuple):
    page_table: jax.Array
    top_flat: jax.Array
    pool_k: jax.Array
    pool_v: jax.Array
    q_col: jax.Array
    k_col: jax.Array
    v_col: jax.Array
    hps: int


N_PG_MOBA = 2 * MOBA_TOPK


def _moba_decode_copies(pt_ref, top_ref, step, slot, pools, bufs, sems, *, hps):
    gps = N_HEADS // hps
    seq, g = step // gps, step % gps
    copies = []
    for hl in range(hps):
        h = g * hps + hl
        rows = pl.ds(pl.multiple_of(h * HEAD_DIM, HEAD_DIM), HEAD_DIM)
        for r in range(MOBA_TOPK):
            blk = top_ref[(seq * N_HEADS + h) * MOBA_TOPK + r]
            for e in range(2):
                page = pt_ref[seq, 2 * blk + e]
                i = (hl * MOBA_TOPK + r) * 2 + e
                for pool, buf, sem in zip(pools, bufs, sems):
                    copies.append(pltpu.make_async_copy(pool.at[page, rows], buf.at[slot, i], sem.at[slot]))
    return copies


def _moba_decode_step(pt_ref, top_ref, step, n_steps, pools, q_ref, kn_ref, vn_ref, o_ref, bufs, sems, *, hps):
    kbuf, vbuf = bufs
    slot = _double_buffer(step, n_steps, lambda s, half: _moba_decode_copies(
        pt_ref, top_ref, s, half, pools, bufs, sems, hps=hps))
    for hl in range(hps):
        dims = slice(hl * HEAD_DIM, (hl + 1) * HEAD_DIM)
        pages = range(hl * N_PG_MOBA, (hl + 1) * N_PG_MOBA)
        q = q_ref[0, dims, :] * SCALE
        logits = [jnp.sum(kbuf[slot, i] * q, axis=0, keepdims=True) for i in pages]
        l_new = jnp.sum(q * kn_ref[0, dims, :], axis=0, keepdims=True)
        s_max = logits[0]
        for s in logits[1:]:
            s_max = jnp.maximum(s_max, s)
        m = jnp.maximum(l_new, jnp.max(s_max, axis=1, keepdims=True))
        p_new = jnp.exp(l_new - m)
        p_sum = jnp.zeros((1, PAGE_SIZE), F32)
        pv = jnp.zeros((HEAD_DIM, PAGE_SIZE), F32)
        for i, s in zip(pages, logits):
            p = jnp.exp(s - m)
            p_sum = p_sum + p
            pv = pv + vbuf[slot, i] * p
        denom = p_new + jnp.sum(p_sum, axis=1, keepdims=True)
        acc = p_new * vn_ref[0, dims, :] + jnp.sum(pv, axis=1, keepdims=True)
        o_ref[0, dims, :] = acc / denom


def _fox_q8(q_row):
    head = lax.broadcasted_iota(jnp.int32, (N_HEADS, 1), 0)
    lane = lax.broadcasted_iota(jnp.int32, (1, W_ATT), 1)
    own = (lane >= head * HEAD_DIM) & (lane < (head + 1) * HEAD_DIM)
    return jnp.where(own, q_row * SCALE, 0.0), own


def _fox_decode_main(pt_ref, step, n_steps, pools, in_refs, sc_refs, *, spd, pg):
    q_ref, _, _, cn_ref, u_ref = in_refs
    m_ref, l_ref, acc_ref, sfx_ref, kbuf, vbuf, lfbuf, ksem, vsem, lfsem = sc_refs
    slot = _stream_pages(pt_ref, step, n_steps, pools, (kbuf, vbuf, lfbuf), (ksem, vsem, lfsem),
                         spd=spd, pg=pg, last_first=True)

    @pl.when(step % spd == 0)
    def _():
        m_ref[...] = jnp.full_like(m_ref, NEG)
        l_ref[...] = jnp.zeros_like(l_ref)
        acc_ref[...] = jnp.zeros_like(acc_ref)
        sfx_ref[...] = jnp.zeros_like(sfx_ref)

    q8b = _bf(_fox_q8(q_ref[pl.ds(step // spd, 1), :])[0])
    cn = cn_ref[0]
    m = m_ref[:, 0:1]
    l = l_ref[:, 0:1]
    sfx = sfx_ref[...]
    lfts = [lfbuf[slot, i] for i in range(pg)]
    within = _dot_right_exact(jnp.concatenate(lfts, axis=0), u_ref[...])
    logits = []
    for i in range(pg):
        r = within[i * N_HEADS:(i + 1) * N_HEADS, :] + sfx
        logits.append(_dot(q8b, _bf(kbuf[slot, i])) + cn + r)
        sfx = sfx + jnp.sum(lfts[i], axis=1, keepdims=True)
    s_max = logits[0]
    for s in logits[1:]:
        s_max = jnp.maximum(s_max, s)
    m_new = jnp.maximum(m, jnp.max(s_max, axis=1, keepdims=True))
    alpha = jnp.exp(m - m_new)
    p_sum = jnp.zeros((N_HEADS, PAGE_SIZE), F32)
    pv = jnp.zeros((N_HEADS, W_ATT), F32)
    for i, s in enumerate(logits):
        p = jnp.exp(s - m_new)
        p_sum = p_sum + p
        pv = pv + _dot_nt(_bf(p), _bf(vbuf[slot, i]))
    l = alpha * l + jnp.sum(p_sum, axis=1, keepdims=True)
    acc = alpha * acc_ref[...] + pv
    m = m_new
    m_ref[...] = jnp.broadcast_to(m, m_ref.shape)
    l_ref[...] = jnp.broadcast_to(l, l_ref.shape)
    acc_ref[...] = acc
    sfx_ref[...] = sfx


def _fox_decode_final(seq, in_refs, o_ref, sc_refs):
    q_ref, kn_ref, vn_ref, cn_ref, _ = in_refs[-5:]
    m_ref, l_ref, acc_ref = sc_refs[:3]
    row = pl.ds(seq, 1)
    q8, own = _fox_q8(q_ref[row, :])
    m = m_ref[:, 0:1]
    cn1 = cn_ref[0][:, 0:1]
    l_new = jnp.sum(q8 * kn_ref[row, :], axis=1, keepdims=True) + cn1 - cn1
    m_fin = jnp.maximum(m, l_new)
    a = jnp.exp(m - m_fin)
    p_new = jnp.exp(l_new - m_fin)
    o8 = (a * acc_ref[...] + p_new * vn_ref[row, :]) / (a * l_ref[:, 0:1] + p_new)
    o_ref[0] = jnp.sum(jnp.where(own, o8, 0.0), axis=0, keepdims=True)


def _fox_decode_stream(page_table, pool_k, pool_v, pool_lft, q, k_new, v_new, cn, *, pg):
    db, n_pages = page_table.shape
    assert n_pages % pg == 0

    spd = n_pages // pg
    vec_spec = pl.BlockSpec((1, 1, W_ATT), lambda b, j, pt: (b, 0, 0))
    u = jnp.asarray(np.tril(np.ones((PAGE_SIZE, PAGE_SIZE), np.float32), -1), BF16)
    page_buf = pltpu.VMEM((2, pg, W_ATT, PAGE_SIZE), F32)
    return _DecodeStream(
        steps_per_seq=spd,
        pools=(pool_k, pool_v, pool_lft),
        in_specs=[pl.BlockSpec((db, W_ATT), lambda b, j, pt: (0, 0))] * 3 + [
            pl.BlockSpec((1, N_HEADS, LANES), lambda b, j, pt: (b, 0, 0)),
            pl.BlockSpec((PAGE_SIZE, PAGE_SIZE), lambda b, j, pt: (0, 0))],
        args=(q, k_new, v_new, cn, u),
        out_spec=vec_spec,
        out_shape=jax.ShapeDtypeStruct((db, 1, W_ATT), F32),
        scratch=[pltpu.VMEM((N_HEADS, LANES), F32), pltpu.VMEM((N_HEADS, LANES), F32),
                 pltpu.VMEM((N_HEADS, W_ATT), F32), pltpu.VMEM((N_HEADS, LANES), F32),
                 page_buf, page_buf, pltpu.VMEM((2, pg, N_HEADS, PAGE_SIZE), F32)]
        + [pltpu.SemaphoreType.DMA((2,))] * 3,
        main=functools.partial(_fox_decode_main, spd=spd, pg=pg),
        final=_fox_decode_final,
    )


def _rope_table(pos):
    half = ROPE_DIMS // 2
    inv = ROPE_THETA ** (-jnp.arange(0, ROPE_DIMS, 2, dtype=F32) / ROPE_DIMS)
    ang = pos.astype(F32)[:, None] * inv[None, :]
    d = np.arange(LANES) % HEAD_DIM
    sel = jnp.asarray(d % half)
    cos = jnp.where(jnp.asarray(d < ROPE_DIMS)[None, :], jnp.cos(ang)[:, sel], 1.0)
    sin = jnp.sin(ang)[:, sel]
    sin_up = jnp.where(jnp.asarray(d < half)[None, :], -sin, 0.0)
    sin_dn = jnp.where(jnp.asarray((d >= half) & (d < ROPE_DIMS))[None, :], sin, 0.0)
    return jnp.concatenate([cos, sin_up, sin_dn], axis=1)


def kernel(x_prompt, x_sample, cache_moba_k, cache_moba_v, cache_fox_k, cache_fox_v, cache_fox_logf,
           state_ffn_conv, page_table, p_prompt, p_sample, norm_attn_g, w_in, b_forget,
           qnorm_moba, knorm_moba, qnorm_fox, knorm_fox, w_branch_moba, w_branch_fox, w_out,
           norm_ffn_g, w_up, w_gate, conv_w, conv_b, w_down, norm_ple_g, w_ple, w_ple_gate):
    b, t_len, d_model = x_prompt.shape
    db, ts, _ = x_sample.shape
    depth = w_in.shape[0]
    n_pages = page_table.shape[1]
    n_pool = cache_moba_k.shape[1]
    assert depth == 1 and ts == 1 and t_len % ATT_TILE == 0 and n_pages % 2 == 0
    past_len = n_pages * PAGE_SIZE
    d_ff = w_up.shape[2]
    n_att = 6 * W_ATT

    w_in0 = w_in[0]
    w_fl = jnp.pad(w_in0[:, n_att:n_att + N_HEADS], ((0, 0), (0, LANES - N_HEADS)))
    w1 = _bf(jnp.concatenate([w_in0[:, :n_att], w_in0[:, n_att + N_HEADS:], w_fl], axis=1))
    bfl = jnp.pad(b_forget[0], (0, LANES - N_HEADS))[None, :]
    gains = jnp.stack([jnp.tile(g[0], N_HEADS) for g in (qnorm_moba, knorm_moba, qnorm_fox, knorm_fox)])
    gains = jnp.pad(gains, ((0, SUBLANES - 4), (0, 0)))
    seg_np = (np.arange(W_ATT)[:, None] // HEAD_DIM == np.arange(LANES)[None, :]).astype(np.float32)
    seg = jnp.asarray(seg_np, BF16)
    segt = jnp.asarray(np.concatenate([seg_np.T, seg_np.T], axis=0), BF16)
    g_attn = norm_attn_g[0][None, :]
    ffn_w = (_bf(w_branch_moba[0]), _bf(w_branch_fox[0]), _bf(w_out[0]),
             norm_ffn_g[0][None, :], _bf(jnp.concatenate([w_up[0], w_gate[0]], axis=1)),
             jnp.pad(conv_w[0], ((0, SUBLANES - CONV_W), (0, 0))), conv_b[0][None, :], _bf(w_down[0]),
             norm_ple_g[0][None, :], _bf(w_ple_gate[0]), _bf(w_ple[0]))

    tm = 256
    xp = x_prompt.reshape(b * t_len, d_model)
    rope_p = _rope_table(jnp.arange(t_len, dtype=jnp.int32))
    qa, ka, va, qb, kb, vb, lf, c, sga, sgb = _inproj(
        xp, g_attn, w1, bfl, gains, rope_p, seg, segt, tm=tm, tiles_per_seq=t_len // tm, kv_transposed=True)
    xs = x_sample.reshape(db, d_model)
    rope_s = _rope_table(jnp.full((db,), past_len, jnp.int32))
    qa_s, ka_s, va_s, qb_s, kb_s, vb_s, lf_s, _, sga_s, sgb_s = _inproj(
        xs, g_attn, w1, bfl, gains, rope_s, seg, segt, tm=db, tiles_per_seq=1, kv_transposed=False)

    attn_steps = b * (W_ATT // LANES) * (t_len // ATT_TILE)
    assert attn_steps % db == 0 and n_pages % (attn_steps // db) == 0
    pg = n_pages // (attn_steps // db)
    col = lambda a: a.reshape(db, W_ATT, 1)
    pool = lambda cache: cache[0].transpose(0, 2, 3, 1).reshape(n_pool, W_ATT, PAGE_SIZE)
    pool_mk = pool(cache_moba_k)
    r3 = lambda a: a.reshape(b, t_len, a.shape[-1])
    ya, top = _prompt_attention(page_table, _moba_score_stream(page_table, pool_mk, col(qa_s), pg=pg),
                                r3(qa), ka, va)
    c3 = r3(c)
    c_t = c3[:, :, :N_HEADS].transpose(0, 2, 1)
    cn = jnp.broadcast_to(lf_s[:, :N_HEADS, None], (db, N_HEADS, LANES))
    fox_stream = _fox_decode_stream(page_table, pool(cache_fox_k), pool(cache_fox_v),
                                    cache_fox_logf[0].transpose(0, 2, 1), qb_s, kb_s, vb_s, cn, pg=pg)
    yb, yb_s = _prompt_attention(page_table, fox_stream, r3(qb), kb, vb, c3, c_t)
    top_flat = top[:, :, :MOBA_TOPK].reshape(-1)

    mlp_steps = b * t_len // tm
    assert (N_HEADS * db) % mlp_steps == 0
    rider = _MobaDecodeRider(page_table, top_flat, pool_mk, pool(cache_moba_v), col(qa_s), col(ka_s), col(va_s),
                             hps=N_HEADS * db // mlp_steps)
    y_p, tail_p, ya_s = _ffn(xp, ya.reshape(b * t_len, W_ATT), yb.reshape(b * t_len, W_ATT), sga, sgb,
                             p_prompt[0].reshape(b * t_len, -1), None, ffn_w, n_seq=b, tm=tm, rider=rider)
    bufs = (state_ffn_conv[0, :, 0, :], state_ffn_conv[0, :, 1, :])
    y_s, a_s = _ffn(xs, ya_s.reshape(db, W_ATT), yb_s.reshape(db, W_ATT), sga_s, sgb_s,
                    p_sample[0].reshape(db, -1), bufs, ffn_w, n_seq=db, tm=db)

    heads_t = lambda a: a.reshape(1, b, N_HEADS, HEAD_DIM, t_len).transpose(0, 1, 4, 2, 3)
    heads = lambda a: a.reshape(1, db, ts, N_HEADS, HEAD_DIM)
    outs_p = (heads_t(ka), heads_t(va), heads_t(kb), heads_t(vb),
              lf[:, :N_HEADS].reshape(1, b, t_len, N_HEADS), tail_p[None, :, SUBLANES - (CONV_W - 1):, :])
    conv_s = jnp.stack([state_ffn_conv[0, :, 1, :], a_s], axis=1)[None]
    outs_s = (heads(ka_s), heads(va_s), heads(kb_s), heads(vb_s),
              lf_s[:, :N_HEADS].reshape(1, db, ts, N_HEADS), conv_s)
    return (y_p.reshape(b, t_len, d_model), y_s.reshape(db, ts, d_model)) + outs_p + outs_s
```
